```python
import math
import jax, jax.numpy as jnp
from jax import lax
import numpy as np


D_MODEL = 1024
BATCH = 8
SEQ = 4096
DEPTH = 2

PLE_DIM = 256
MLA_HEADS = 4
MLA_NOPE = 64
MLA_ROPE = 32
MLA_V = 64
MLA_Q_RANK = 192
MLA_KV_RANK = 128
MLA_OUT = MLA_HEADS * MLA_V
FOX_HEADS = 4
FOX_HEAD_DIM = 64
FOX_OUT = FOX_HEADS * FOX_HEAD_DIM
LRU_WIDTH = 512
LRU_BLOCKS = 8
LRU_BLOCK = LRU_WIDTH // LRU_BLOCKS
LRU_CONV = 4
LRU_C = 8.0
D_MIX = MLA_OUT + FOX_OUT + LRU_WIDTH
IN_SIZES = (MLA_Q_RANK, MLA_KV_RANK, MLA_ROPE, FOX_OUT, FOX_OUT, FOX_OUT, FOX_HEADS, LRU_WIDTH, LRU_WIDTH)
D_IN = MLA_Q_RANK + MLA_KV_RANK + MLA_ROPE + 3 * FOX_OUT + FOX_HEADS + 2 * LRU_WIDTH
D_FF = 2816
FFN_CONV = 3
ROPE_THETA = 10000.0
EPS = 1e-6
Q_BLOCK = 128

kernel_name = 'hybrid_mla_fox_rglru_convffn_ple'


def _offsets(sizes):
    out, acc = [], 0
    for s in sizes[:-1]:
        acc += s
        out.append(acc)
    return out


def rms_norm(x, g):
    xf = x.astype(jnp.float32)
    y = xf * lax.rsqrt(jnp.mean(xf * xf, axis=-1, keepdims=True) + EPS)
    return (y * g.astype(jnp.float32)).astype(x.dtype)


def rope(x, positions):
    half = x.shape[-1] // 2
    freqs = ROPE_THETA ** (-jnp.arange(half, dtype=jnp.float32) / half)
    ang = positions.astype(jnp.float32)[..., None] * freqs
    ang = ang.reshape(ang.shape[:2] + (1,) * (x.ndim - 3) + (half,))
    cos, sin = jnp.cos(ang), jnp.sin(ang)
    xf = x.astype(jnp.float32)
    x1, x2 = xf[..., :half], xf[..., half:]
    return jnp.concatenate([x1 * cos - x2 * sin, x2 * cos + x1 * sin], axis=-1).astype(x.dtype)


def causal_dwconv(x, w, b):
    K = w.shape[0]
    S = x.shape[1]
    xp = jnp.pad(x, ((0, 0), (K - 1, 0), (0, 0)))
    out = b + xp[:, 0:S] * w[0]
    for k in range(1, K):
        out = out + xp[:, k:k + S] * w[k]
    return out


def causal_block_attention(q, k, v, scale, decay=None):
    B, S, H, dk = q.shape
    dv = v.shape[-1]
    nb = S // Q_BLOCK
    qb = q.reshape(B, nb, Q_BLOCK, H, dk).transpose(1, 0, 3, 2, 4)
    kh = k.transpose(0, 2, 1, 3)
    vh = v.transpose(0, 2, 1, 3)
    k_pos = jnp.arange(S)
    blk_idx = jnp.arange(nb)
    ck = None if decay is None else decay.transpose(0, 2, 1)

    def block(q_blk, c_blk, idx):
        s = jnp.einsum('bhqd,bhkd->bhqk', q_blk, kh, preferred_element_type=jnp.float32) * scale
        if c_blk is not None:
            s = s + c_blk[..., :, None] - ck[..., None, :]
        q_pos = idx * Q_BLOCK + jnp.arange(Q_BLOCK)
        s = jnp.where(k_pos[None, :] <= q_pos[:, None], s, -jnp.inf)
        pr = jax.nn.softmax(s, axis=-1).astype(vh.dtype)
        return jnp.einsum('bhqk,bhkd->bhqd', pr, vh)

    if decay is None:
        out = lax.map(lambda a: block(a[0], None, a[1]), (qb, blk_idx))
    else:
        cb = ck.reshape(B, H, nb, Q_BLOCK).transpose(2, 0, 1, 3)
        out = lax.map(lambda a: block(a[0], a[1], a[2]), (qb, cb, blk_idx))
    return out.transpose(1, 0, 3, 2, 4).reshape(B, S, H, dv)


def _linear_combine(left, right):
    a_l, b_l = left
    a_r, b_r = right
    return a_l * a_r, a_r * b_l + b_r


def hybrid_mixer(xn, positions, w_in, g_qc, w_uq, g_kvc, w_ukv, b_f, lru_conv_w, lru_conv_b,
                 w_r, b_r, w_i, b_i, lru_lambda, g_out, w_o):
    B, S, _ = xn.shape
    z = xn @ w_in
    q_c, kv_c, k_r, fq, fk, fv, f_logit, lx, lg = jnp.split(z, _offsets(IN_SIZES), axis=-1)

    q = (rms_norm(q_c, g_qc) @ w_uq).reshape(B, S, MLA_HEADS, MLA_NOPE + MLA_ROPE)
    q = jnp.concatenate([q[..., :MLA_NOPE], rope(q[..., MLA_NOPE:], positions)], axis=-1)
    kv = (rms_norm(kv_c, g_kvc) @ w_ukv).reshape(B, S, MLA_HEADS, MLA_NOPE + MLA_V)
    k_nope, v_mla = kv[..., :MLA_NOPE], kv[..., MLA_NOPE:]
    k_rope = rope(k_r, positions)
    k = jnp.concatenate([k_nope, jnp.broadcast_to(k_rope[:, :, None, :], (B, S, MLA_HEADS, MLA_ROPE))], axis=-1)
    o_mla = causal_block_attention(q, k, v_mla, (MLA_NOPE + MLA_ROPE) ** -0.5).reshape(B, S, MLA_OUT)

    log_f = jax.nn.log_sigmoid(f_logit.astype(jnp.float32) + b_f.astype(jnp.float32))
    c = jnp.cumsum(log_f, axis=1)
    o_fox = causal_block_attention(fq.reshape(B, S, FOX_HEADS, FOX_HEAD_DIM),
                                   fk.reshape(B, S, FOX_HEADS, FOX_HEAD_DIM),
                                   fv.reshape(B, S, FOX_HEADS, FOX_HEAD_DIM),
                                   FOX_HEAD_DIM ** -0.5, decay=c).reshape(B, S, FOX_OUT)

    xc = causal_dwconv(lx, lru_conv_w, lru_conv_b)
    xblk = xc.reshape(B, S, LRU_BLOCKS, LRU_BLOCK)
    r = jax.nn.sigmoid(jnp.einsum('bsnc,ncd->bsnd', xblk, w_r).reshape(B, S, LRU_WIDTH) + b_r)
    i = jax.nn.sigmoid(jnp.einsum('bsnc,ncd->bsnd', xblk, w_i).reshape(B, S, LRU_WIDTH) + b_i)
    log_a = -LRU_C * r.astype(jnp.float32) * jax.nn.softplus(-lru_lambda.astype(jnp.float32))
    a_t = jnp.exp(log_a)
    bx = jnp.sqrt(-jnp.expm1(2.0 * log_a)) * (i * xc).astype(jnp.float32)
    _, h = lax.associative_scan(_linear_combine, (a_t, bx), axis=1)
    o_lru = h.astype(xn.dtype) * jax.nn.gelu(lg)

    o = jnp.concatenate([
        rms_norm(o_mla, g_out[:MLA_OUT]),
        rms_norm(o_fox, g_out[MLA_OUT:MLA_OUT + FOX_OUT]),
        rms_norm(o_lru, g_out[MLA_OUT + FOX_OUT:]),
    ], axis=-1)
    return o @ w_o


def conv_ffn(xn, w_up, ffn_conv_w, ffn_conv_b, w_down):
    u = causal_dwconv(xn @ w_up, ffn_conv_w, ffn_conv_b)
    g, v = jnp.split(u, 2, axis=-1)
    return (jax.nn.silu(g) * v) @ w_down


def per_layer_embedding(h, p_i, g_ple, w_ple_gate, w_ple_proj):
    return jax.nn.sigmoid(rms_norm(h, g_ple) @ w_ple_gate) * (p_i @ w_ple_proj)


def setup_inputs(seed: int = 0) -> dict:
    key = jax.random.key(seed)
    ks = iter(jax.random.split(key, 40))

    def nrm(shape, scale):
        return jax.random.normal(next(ks), shape, jnp.float32) * scale

    def gain(shape):
        return 1.0 + nrm(shape, 0.02)

    x = nrm((BATCH, SEQ, D_MODEL), 1.0)
    p = nrm((DEPTH, BATCH, SEQ, PLE_DIM), 1.0)
    offset = jax.random.randint(next(ks), (BATCH, 1), 0, 1024, dtype=jnp.int32)
    positions = (offset + jnp.arange(SEQ, dtype=jnp.int32)[None, :]).astype(jnp.int32)

    u = jax.random.uniform(next(ks), (DEPTH, LRU_WIDTH), jnp.float32, 0.9, 0.999)
    s = u ** (1.0 / LRU_C)
    lru_lambda = jnp.log(s) - jnp.log1p(-s)

    return {
        'x': x,
        'p': p,
        'positions': positions,
        'g_mix': gain((DEPTH, D_MODEL)),
        'w_in': nrm((DEPTH, D_MODEL, D_IN), D_MODEL ** -0.5),
        'g_qc': gain((DEPTH, MLA_Q_RANK)),
        'w_uq': nrm((DEPTH, MLA_Q_RANK, MLA_HEADS * (MLA_NOPE + MLA_ROPE)), MLA_Q_RANK ** -0.5),
        'g_kvc': gain((DEPTH, MLA_KV_RANK)),
        'w_ukv': nrm((DEPTH, MLA_KV_RANK, MLA_HEADS * (MLA_NOPE + MLA_V)), MLA_KV_RANK ** -0.5),
        'b_f': jax.random.uniform(next(ks), (DEPTH, FOX_HEADS), jnp.float32, 1.0, 4.0),
        'lru_conv_w': nrm((DEPTH, LRU_CONV, LRU_WIDTH), LRU_CONV ** -0.5),
        'lru_conv_b': nrm((DEPTH, LRU_WIDTH), 0.02),
        'w_r': nrm((DEPTH, LRU_BLOCKS, LRU_BLOCK, LRU_BLOCK), LRU_BLOCK ** -0.5),
        'b_r': nrm((DEPTH, LRU_WIDTH), 0.02),
        'w_i': nrm((DEPTH, LRU_BLOCKS, LRU_BLOCK, LRU_BLOCK), LRU_BLOCK ** -0.5),
        'b_i': nrm((DEPTH, LRU_WIDTH), 0.02),
        'lru_lambda': lru_lambda,
        'g_out': gain((DEPTH, D_MIX)),
        'w_o': nrm((DEPTH, D_MIX, D_MODEL), D_MIX ** -0.5),
        'g_ffn': gain((DEPTH, D_MODEL)),
        'w_up': nrm((DEPTH, D_MODEL, 2 * D_FF), D_MODEL ** -0.5),
        'ffn_conv_w': nrm((DEPTH, FFN_CONV, 2 * D_FF), FFN_CONV ** -0.5),
        'ffn_conv_b': nrm((DEPTH, 2 * D_FF), 0.02),
        'w_down': nrm((DEPTH, D_FF, D_MODEL), D_FF ** -0.5),
        'g_ple': gain((DEPTH, D_MODEL)),
        'w_ple_gate': nrm((DEPTH, D_MODEL, D_MODEL), D_MODEL ** -0.5),
        'w_ple_proj': nrm((DEPTH, PLE_DIM, D_MODEL), PLE_DIM ** -0.5),
        'g_final': gain((D_MODEL,)),
    }


def reference(x, p, positions, g_mix, w_in, g_qc, w_uq, g_kvc, w_ukv, b_f, lru_conv_w, lru_conv_b,
              w_r, b_r, w_i, b_i, lru_lambda, g_out, w_o, g_ffn, w_up, ffn_conv_w, ffn_conv_b,
              w_down, g_ple, w_ple_gate, w_ple_proj, g_final):
    h = x
    for l in range(DEPTH):
        h = h + hybrid_mixer(rms_norm(h, g_mix[l]), positions, w_in[l], g_qc[l], w_uq[l], g_kvc[l],
                             w_ukv[l], b_f[l], lru_conv_w[l], lru_conv_b[l], w_r[l], b_r[l], w_i[l],
                             b_i[l], lru_lambda[l], g_out[l], w_o[l])
        h = h + conv_ffn(rms_norm(h, g_ffn[l]), w_up[l], ffn_conv_w[l], ffn_conv_b[l], w_down[l])
        h = h + per_layer_embedding(h, p[l], g_ple[l], w_ple_gate[l], w_ple_proj[l])
    return rms_norm(h, g_final)
```

```python
import functools
import math

import jax
import jax.numpy as jnp
from jax import lax
from jax.experimental import pallas as pl
from jax.experimental.pallas import tpu as pltpu

F32 = jnp.float32
BF16 = jnp.bfloat16

D_MODEL = 1024
PLE_DIM = 256
MLA_HEADS = 4
MLA_NOPE = 64
MLA_ROPE = 32
MLA_HALF = MLA_ROPE // 2
MLA_V = 64
MLA_Q_RANK = 192
MLA_KV_RANK = 128
FOX_HEADS = 4
FOX_HEAD_DIM = 64
FOX_OUT = FOX_HEADS * FOX_HEAD_DIM
LRU_WIDTH = 512
LRU_BLOCKS = 8
LRU_BLOCK = LRU_WIDTH // LRU_BLOCKS
LRU_CONV = 4
LRU_C = 8.0
D_FF = 2816
FFN_CONV = 3
ROPE_THETA = 10000.0
EPS = 1e-6
LOG2E = math.log2(math.e)

LANES = 128
SUBLANES = 8
V7X_VMEM_BYTES = 64 << 20

ROW_TILE = 512
ATT_BLOCK = 512
FF_CHUNK = 256
N_FF_CHUNKS = D_FF // FF_CHUNK

ZC_G0 = 0
ZC_KV = 256
ZC_FOX = 512
ZC_LRU = 1280
ZC_END = 2304
FLOGIT_LANE = MLA_Q_RANK - LANES

MLA_QSCALE = (MLA_NOPE + MLA_ROPE) ** -0.5 * LOG2E
FOX_QSCALE = FOX_HEAD_DIM ** -0.5 * LOG2E


def _vmem_limit(nbytes):
    return int(min(nbytes + (8 << 20), V7X_VMEM_BYTES - (4 << 20)))


def _const_spec(shape):
    nd = len(shape)
    return pl.BlockSpec(shape, lambda *_: (0,) * nd, pipeline_mode=pl.Buffered(1))


def _rms(x, gain, n=None):
    n = x.shape[-1] if n is None else n
    return x * lax.rsqrt(jnp.sum(x * x, axis=-1, keepdims=True) * (1.0 / n) + EPS) * gain


def _dot(a, b):
    return jnp.dot(a, b, preferred_element_type=F32)


def _softplus(x):
    return jnp.maximum(x, 0.0) + jnp.log1p(jnp.exp(-jnp.abs(x)))


def _rope_table_kernel(pos_ref, freq_ref, sign_ref, cos_ref, sin_ref):
    ang = pos_ref[...].astype(F32) * freq_ref[...]
    cos_ref[...] = jnp.cos(ang)
    sin_ref[...] = jnp.sin(ang) * sign_ref[...]


def _rope_tables(positions):
    t = positions.size
    half = MLA_HALF
    freqs = ROPE_THETA ** (-jnp.arange(half, dtype=F32) / half)
    freq_row = jnp.tile(freqs, LANES // half)[None, :]
    sign_row = jnp.where(jnp.arange(LANES) < LANES // 2, -1.0, 1.0).astype(F32)[None, :]
    tm = ROW_TILE
    return pl.pallas_call(
        _rope_table_kernel,
        grid=(t // tm,),
        in_specs=[pl.BlockSpec((tm, 1), lambda i: (i, 0)),
                  _const_spec((1, LANES)), _const_spec((1, LANES))],
        out_specs=[pl.BlockSpec((tm, LANES), lambda i: (i, 0))] * 2,
        out_shape=[jax.ShapeDtypeStruct((t, LANES), F32)] * 2,
        name="rope_tables",
    )(positions.reshape(t, 1), freq_row, sign_row)


def _proj_kernel(h_ref, gmix_ref, win_ref, gqc_ref, wq_ref, gkvc_ref, wkv_ref, cos_ref, sin_ref,
                 qm_ref, km_ref, vm_ref, fq_ref, fk_ref, fv_ref, fl_ref, lx_ref, lg_ref):
    xn = _rms(h_ref[...], gmix_ref[...]).astype(BF16)
    cos = cos_ref[...]
    sin = sin_ref[...]

    def zcols(lo, hi):
        return _dot(xn, win_ref[:, lo:hi])

    def rotate(r):
        return r * cos + pltpu.roll(r, LANES // 2, axis=1) * sin

    g0 = zcols(ZC_G0, ZC_KV)
    lane = lax.broadcasted_iota(jnp.int32, g0.shape, 1)
    qc = jnp.where(lane < MLA_Q_RANK, g0, 0.0)
    qcn = _rms(qc, gqc_ref[...], MLA_Q_RANK).astype(BF16)
    qq = _dot(qcn, wq_ref[...]) * MLA_QSCALE
    q_rot = rotate(qq[:, 2 * LANES:]).astype(BF16)
    qm_ref[:, 0:LANES] = qq[:, 0:LANES].astype(BF16)
    qm_ref[:, LANES:2 * LANES] = q_rot
    qm_ref[:, 2 * LANES:3 * LANES] = qq[:, LANES:2 * LANES].astype(BF16)
    qm_ref[:, 3 * LANES:] = q_rot
    fl_ref[...] = g0[:, LANES:]

    kvz = zcols(ZC_KV, ZC_FOX)
    kvn = _rms(kvz[:, :MLA_KV_RANK], gkvc_ref[...]).astype(BF16)
    kv = _dot(kvn, wkv_ref[...])
    k_rot = rotate(kvz[:, MLA_KV_RANK:]).astype(BF16)
    km_ref[:, 0:LANES] = kv[:, 0:LANES].astype(BF16)
    km_ref[:, LANES:2 * LANES] = k_rot
    km_ref[:, 2 * LANES:3 * LANES] = kv[:, LANES:2 * LANES].astype(BF16)
    km_ref[:, 3 * LANES:] = k_rot
    vm_ref[...] = kv[:, 2 * LANES:].astype(BF16)

    f = zcols(ZC_FOX, ZC_LRU)
    fq_ref[...] = (f[:, :FOX_OUT] * FOX_QSCALE).astype(BF16)
    fk_ref[...] = f[:, FOX_OUT:2 * FOX_OUT].astype(BF16)
    fv_ref[...] = f[:, 2 * FOX_OUT:].astype(BF16)

    lz = zcols(ZC_LRU, ZC_END)
    lx_ref[...] = lz[:, :LRU_WIDTH]
    lg_ref[...] = lz[:, LRU_WIDTH:]


def _proj(h, gmix, win, gqc, wq, gkvc, wkv, cos, sin):
    t = h.shape[0]
    tm = ROW_TILE
    row = lambda c: pl.BlockSpec((tm, c), lambda i: (i, 0))
    outs = [(4 * LANES, BF16), (4 * LANES, BF16), (2 * LANES, BF16),
            (FOX_OUT, BF16), (FOX_OUT, BF16), (FOX_OUT, BF16),
            (LANES, F32), (LRU_WIDTH, F32), (LRU_WIDTH, F32)]
    weights = (gmix, win, gqc, wq, gkvc, wkv)
    return pl.pallas_call(
        _proj_kernel,
        grid=(t // tm,),
        in_specs=[row(D_MODEL)] + [_const_spec(w.shape) for w in weights] + [row(LANES), row(LANES)],
        out_specs=[row(c) for c, _ in outs],
        out_shape=[jax.ShapeDtypeStruct((t, c), d) for c, d in outs],
        compiler_params=pltpu.CompilerParams(
            dimension_semantics=("parallel",), vmem_limit_bytes=_vmem_limit(40 << 20)),
        name="proj",
    )(h, *weights, cos, sin)


def _scan_kernel(fl_ref, lx_ref, lg_ref, bf_ref, cw_ref, cb_ref, wri_ref, br_ref, bi_ref, lam_ref,
                 g_ref, cq_ref, ck_ref, o_ref, xbuf, hcar, ccar):
    tm = lx_ref.shape[0]
    halo = SUBLANES

    @pl.when(pl.program_id(1) == 0)
    def _():
        xbuf[0:halo, :] = jnp.zeros((halo, LRU_WIDTH), F32)
        hcar[...] = jnp.zeros_like(hcar)
        ccar[...] = jnp.zeros_like(ccar)

    lane = lax.broadcasted_iota(jnp.int32, (tm, LANES), 1)
    row = lax.broadcasted_iota(jnp.int32, (tm, LANES), 0)
    valid = (lane >= FLOGIT_LANE) & (lane < FLOGIT_LANE + FOX_HEADS)
    c = jnp.where(valid, -_softplus(-(fl_ref[...] + bf_ref[...])), 0.0)
    shift = 1
    while shift < tm:
        c = c + jnp.where(row >= shift, pltpu.roll(c, shift, axis=0), 0.0)
        shift *= 2
    c = c + ccar[...]
    ccar[...] = c[tm - 1:tm, :]
    c2 = c * LOG2E
    cq_ref[...] = c2
    ck_ref[...] = c2.T[FLOGIT_LANE:FLOGIT_LANE + SUBLANES, :]

    xbuf[halo:halo + tm, :] = lx_ref[...]
    xc = cb_ref[...] + cw_ref[LRU_CONV - 1:LRU_CONV, :] * lx_ref[...]
    for k in range(LRU_CONV - 1):
        back = LRU_CONV - 1 - k
        xc = xc + cw_ref[k:k + 1, :] * xbuf[halo - back:halo - back + tm, :]
    xbuf[0:halo, :] = xbuf[tm:tm + halo, :]

    xcb = xc.astype(BF16)
    half = LRU_WIDTH // 2
    gates = [_dot(xcb[:, j * half:(j + 1) * half], wri_ref[j]) for j in range(2)]
    r = jax.nn.sigmoid(jnp.concatenate([g[:, :half] for g in gates], axis=1) + br_ref[...])
    i = jax.nn.sigmoid(jnp.concatenate([g[:, half:] for g in gates], axis=1) + bi_ref[...])
    log_a = (-LRU_C) * r * _softplus(-lam_ref[...])
    a = jnp.exp(log_a)
    b = jnp.sqrt(-jnp.tanh(log_a) * (a * a + 1.0)) * (i * xc)

    rowl = lax.broadcasted_iota(jnp.int32, (tm, LRU_WIDTH), 0)
    shift = 1
    while shift < tm:
        keep = rowl >= shift
        a_sh = jnp.where(keep, pltpu.roll(a, shift, axis=0), 1.0)
        b_sh = jnp.where(keep, pltpu.roll(b, shift, axis=0), 0.0)
        b = a * b_sh + b
        a = a * a_sh
        shift *= 2
    h = b + a * hcar[...]
    hcar[...] = h[tm - 1:tm, :]

    o = h * jax.nn.gelu(lg_ref[...], approximate=True)
    o_ref[...] = _rms(o, g_ref[...]).astype(BF16)


def _scan(fl, lx, lg, bf_row, cw, cb, wri, br, bi, lam, g_lru, batch, seq):
    tm = ATT_BLOCK
    ns = seq // tm
    rows = lambda c: pl.BlockSpec((tm, c), lambda b, s: (b * ns + s, 0))
    weights = (bf_row, cw, cb, wri, br, bi, lam, g_lru)
    t = batch * seq
    return pl.pallas_call(
        _scan_kernel,
        grid=(batch, ns),
        in_specs=[rows(LANES), rows(LRU_WIDTH), rows(LRU_WIDTH)] + [_const_spec(w.shape) for w in weights],
        out_specs=[rows(LANES),
                   pl.BlockSpec((None, None, SUBLANES, tm), lambda b, s: (b, s, 0, 0)),
                   rows(LRU_WIDTH)],
        out_shape=[jax.ShapeDtypeStruct((t, LANES), F32),
                   jax.ShapeDtypeStruct((batch, ns, SUBLANES, tm), F32),
                   jax.ShapeDtypeStruct((t, LRU_WIDTH), BF16)],
        scratch_shapes=[pltpu.VMEM((tm + SUBLANES, LRU_WIDTH), F32),
                        pltpu.VMEM((1, LRU_WIDTH), F32),
                        pltpu.VMEM((1, LANES), F32)],
        compiler_params=pltpu.CompilerParams(
            dimension_semantics=("parallel", "arbitrary"), vmem_limit_bytes=_vmem_limit(32 << 20)),
        name="scan",
    )(fl, lx, lg, *weights)


def _attend(q, load_k, load_v, qi, blk, bias_q=None, load_bias_k=None):
    row = lax.broadcasted_iota(jnp.int32, (blk, blk), 0)
    col = lax.broadcasted_iota(jnp.int32, (blk, blk), 1)

    def step(j, carry, diagonal):
        m, l, acc = carry
        s = lax.dot_general(q, load_k(j), (((1,), (1,)), ((), ())), preferred_element_type=F32)
        if bias_q is not None:
            s = s + bias_q - load_bias_k(j)
        if diagonal:
            s = jnp.where(col <= row, s, -jnp.inf)
        m_new = jnp.maximum(m, jnp.max(s, axis=-1, keepdims=True))
        alpha = jnp.exp2(m - m_new)
        p = jnp.exp2(s - m_new)
        l = alpha * l + jnp.sum(p, axis=-1, keepdims=True)
        acc = alpha * acc + _dot(p.astype(BF16), load_v(j))
        return m_new, l, acc

    init = (jnp.full((blk, 1), -jnp.inf, F32), jnp.zeros((blk, 1), F32), jnp.zeros((blk, LANES), F32))
    carry = lax.fori_loop(0, qi, lambda j, c: step(j, c, False), init)
    _, l, acc = step(qi, carry, True)
    return acc / l


def _pair_norm_store(o_ref, pairs, g_ref):
    n = len(pairs) * LANES
    ssq = sum(jnp.sum(p * p, axis=-1, keepdims=True) for p in pairs)
    rs = lax.rsqrt(ssq * (1.0 / n) + EPS)
    for i, p in enumerate(pairs):
        o_ref[:, i * LANES:(i + 1) * LANES] = (p * rs * g_ref[:, i * LANES:(i + 1) * LANES]).astype(BF16)


def _mla_attn_kernel(q_ref, k_ref, v_ref, g_ref, o_ref):
    blk = q_ref.shape[0]
    qi = pl.program_id(1)
    lane = lax.broadcasted_iota(jnp.int32, (blk, 2 * LANES), 1)
    lane_o = lax.broadcasted_iota(jnp.int32, (blk, LANES), 1)
    pairs = []
    for p in range(MLA_HEADS // 2):
        q_pair = q_ref[:, 2 * LANES * p:2 * LANES * (p + 1)]
        load_k = lambda j, p=p: k_ref[pl.ds(pl.multiple_of(j * blk, blk), blk), 2 * LANES * p:2 * LANES * (p + 1)]
        load_v = lambda j, p=p: v_ref[pl.ds(pl.multiple_of(j * blk, blk), blk), LANES * p:LANES * (p + 1)]
        outs = []
        for e in range(2):
            h = 2 * p + e
            x1 = LANES + MLA_HALF * h
            x2 = x1 + LANES // 2
            mine = (((lane >= MLA_NOPE * e) & (lane < MLA_NOPE * (e + 1)))
                    | ((lane >= x1) & (lane < x1 + MLA_HALF))
                    | ((lane >= x2) & (lane < x2 + MLA_HALF)))
            q = jnp.where(mine, q_pair, jnp.zeros_like(q_pair))
            outs.append(_attend(q, load_k, load_v, qi, blk))
        pairs.append(jnp.where(lane_o < MLA_V, outs[0], outs[1]))
    _pair_norm_store(o_ref, pairs, g_ref)


def _fox_attn_kernel(q_ref, k_ref, v_ref, cq_ref, ck_ref, g_ref, o_ref):
    blk = q_ref.shape[0]
    qi = pl.program_id(1)
    lane = lax.broadcasted_iota(jnp.int32, (blk, LANES), 1)
    pairs = []
    for p in range(FOX_HEADS // 2):
        q_pair = q_ref[:, LANES * p:LANES * (p + 1)]
        load_k = lambda j, p=p: k_ref[pl.ds(pl.multiple_of(j * blk, blk), blk), LANES * p:LANES * (p + 1)]
        load_v = lambda j, p=p: v_ref[pl.ds(pl.multiple_of(j * blk, blk), blk), LANES * p:LANES * (p + 1)]
        outs = []
        for e in range(2):
            h = 2 * p + e
            mine = (lane >= FOX_HEAD_DIM * e) & (lane < FOX_HEAD_DIM * (e + 1))
            q = jnp.where(mine, q_pair, jnp.zeros_like(q_pair))
            bias_q = cq_ref[:, FLOGIT_LANE + h:FLOGIT_LANE + h + 1]
            load_bias_k = lambda j, h=h: ck_ref[j, h:h + 1, :]
            outs.append(_attend(q, load_k, load_v, qi, blk, bias_q, load_bias_k))
        pairs.append(jnp.where(lane < FOX_HEAD_DIM, outs[0], outs[1]))
    _pair_norm_store(o_ref, pairs, g_ref)


def _attention(kernel, name, q, k, v, extra_tiles, extra_full, gain, batch, seq):
    blk = ATT_BLOCK
    nq = seq // blk
    tile = lambda a: pl.BlockSpec((None, blk, a.shape[-1]), lambda b, i: (b, i, 0))
    full = lambda a: pl.BlockSpec((None,) + a.shape[1:], lambda b, i: (b,) + (0,) * (a.ndim - 1))
    q3, k3, v3 = (a.reshape(batch, seq, a.shape[-1]) for a in (q, k, v))
    extra_tiles = [a.reshape(batch, seq, a.shape[-1]) for a in extra_tiles]
    d_out = v.shape[-1]
    out = pl.pallas_call(
        kernel,
        grid=(batch, nq),
        in_specs=([tile(q3), full(k3), full(v3)] + [tile(a) for a in extra_tiles]
                  + [full(a) for a in extra_full] + [_const_spec(gain.shape)]),
        out_specs=pl.BlockSpec((None, blk, d_out), lambda b, i: (b, i, 0)),
        out_shape=jax.ShapeDtypeStruct((batch, seq, d_out), BF16),
        compiler_params=pltpu.CompilerParams(
            dimension_semantics=("parallel", "arbitrary"), vmem_limit_bytes=_vmem_limit(40 << 20)),
        name=name,
    )(q3, k3, v3, *extra_tiles, *extra_full, gain)
    return out.reshape(batch * seq, d_out)


def _oproj_kernel(om_ref, of_ref, ol_ref, h_ref, wo_ref, g_ref, h1_ref, xn_ref):
    a, b = om_ref.shape[1], om_ref.shape[1] + of_ref.shape[1]
    y = (_dot(om_ref[...], wo_ref[0:a, :]) + _dot(of_ref[...], wo_ref[a:b, :])
         + _dot(ol_ref[...], wo_ref[b:, :]))
    h1 = h_ref[...] + y
    h1_ref[...] = h1
    xn_ref[...] = _rms(h1, g_ref[...]).astype(BF16)


def _oproj(om, of, ol, h, wo, g_ffn):
    t = h.shape[0]
    tm = ROW_TILE
    row = lambda c: pl.BlockSpec((tm, c), lambda i: (i, 0))
    return pl.pallas_call(
        _oproj_kernel,
        grid=(t // tm,),
        in_specs=[row(om.shape[1]), row(of.shape[1]), row(ol.shape[1]), row(D_MODEL),
                  _const_spec(wo.shape), _const_spec(g_ffn.shape)],
        out_specs=[row(D_MODEL), row(D_MODEL)],
        out_shape=[jax.ShapeDtypeStruct((t, D_MODEL), F32), jax.ShapeDtypeStruct((t, D_MODEL), BF16)],
        compiler_params=pltpu.CompilerParams(
            dimension_semantics=("parallel",), vmem_limit_bytes=_vmem_limit(24 << 20)),
        name="oproj",
    )(om, of, ol, h, wo, g_ffn)


def _ffn_kernel(xn_ref, h1_ref, p_ref, wg_ref, wv_ref, cwg_ref, cwv_ref, cbg_ref, cbv_ref, wd_ref,
                gple_ref, wgate_ref, wproj_ref, gfin_ref, out_ref,
                ubuf_g, ubuf_v, tail_g, tail_v, acc_ref, *, final):
    tm = xn_ref.shape[0]
    halo = SUBLANES

    @pl.when(pl.program_id(1) == 0)
    def _():
        tail_g[...] = jnp.zeros_like(tail_g)
        tail_v[...] = jnp.zeros_like(tail_v)

    acc_ref[...] = jnp.zeros_like(acc_ref)
    xn = xn_ref[...]

    def conv(u, ubuf, tail, cw, cb):
        ubuf[0:halo, :] = tail
        ubuf[halo:halo + tm, :] = u
        return (cb + cw[0:1, :] * ubuf[halo - 2:halo - 2 + tm, :]
                + cw[1:2, :] * ubuf[halo - 1:halo - 1 + tm, :] + cw[2:3, :] * u)

    def chunk(c, carry):
        ug = _dot(xn, wg_ref[c])
        uv = _dot(xn, wv_ref[c])
        g = conv(ug, ubuf_g, tail_g[c], cwg_ref[c], cbg_ref[c])
        v = conv(uv, ubuf_v, tail_v[c], cwv_ref[c], cbv_ref[c])
        tail_g[c] = ug[tm - halo:, :]
        tail_v[c] = uv[tm - halo:, :]
        act = (g * jax.nn.sigmoid(g) * v).astype(BF16)
        acc_ref[...] += _dot(act, wd_ref[c])
        return carry

    lax.fori_loop(0, N_FF_CHUNKS, chunk, 0)

    h2 = h1_ref[...] + acc_ref[...]
    gate = jax.nn.sigmoid(_dot(_rms(h2, gple_ref[...]).astype(BF16), wgate_ref[...]))
    h3 = h2 + gate * _dot(p_ref[...].astype(BF16), wproj_ref[...])
    out_ref[...] = _rms(h3, gfin_ref[...]) if final else h3


def _ffn(xn, h1, p_l, wg, wv, cwg, cwv, cbg, cbv, wd, gple, wgate, wproj, gfin, batch, seq, final):
    tm = ROW_TILE
    ns = seq // tm
    t = batch * seq
    rows = lambda c: pl.BlockSpec((tm, c), lambda b, s: (b * ns + s, 0))
    weights = (wg, wv, cwg, cwv, cbg, cbv, wd, gple, wgate, wproj, gfin)
    return pl.pallas_call(
        functools.partial(_ffn_kernel, final=final),
        grid=(batch, ns),
        in_specs=[rows(D_MODEL), rows(D_MODEL), rows(PLE_DIM)] + [_const_spec(w.shape) for w in weights],
        out_specs=rows(D_MODEL),
        out_shape=jax.ShapeDtypeStruct((t, D_MODEL), F32),
        scratch_shapes=[pltpu.VMEM((tm + SUBLANES, FF_CHUNK), F32),
                        pltpu.VMEM((tm + SUBLANES, FF_CHUNK), F32),
                        pltpu.VMEM((N_FF_CHUNKS, SUBLANES, FF_CHUNK), F32),
                        pltpu.VMEM((N_FF_CHUNKS, SUBLANES, FF_CHUNK), F32),
                        pltpu.VMEM((tm, D_MODEL), F32)],
        compiler_params=pltpu.CompilerParams(
            dimension_semantics=("parallel", "arbitrary"), vmem_limit_bytes=_vmem_limit(48 << 20)),
        name="ffn_ple",
    )(xn, h1, p_l, *weights)


def _pack_w_in(w):
    o_kvc = MLA_Q_RANK
    o_kr = o_kvc + MLA_KV_RANK
    o_fox = o_kr + MLA_ROPE
    o_fl = o_fox + 3 * FOX_OUT
    o_lru = o_fl + FOX_HEADS
    pad = jnp.zeros((w.shape[0], ZC_KV - MLA_Q_RANK - FOX_HEADS), w.dtype)
    kr1 = jnp.tile(w[:, o_kr:o_kr + MLA_HALF], (1, MLA_HEADS))
    kr2 = jnp.tile(w[:, o_kr + MLA_HALF:o_fox], (1, MLA_HEADS))
    return jnp.concatenate(
        [w[:, :MLA_Q_RANK], w[:, o_fl:o_lru], pad, w[:, o_kvc:o_kr], kr1, kr2, w[:, o_fox:o_fl], w[:, o_lru:]],
        axis=1).astype(BF16)


def _pack_w_uq(w):
    w3 = w.reshape(MLA_Q_RANK, MLA_HEADS, MLA_NOPE + MLA_ROPE)
    nope = w3[:, :, :MLA_NOPE].reshape(MLA_Q_RANK, -1)
    x1 = w3[:, :, MLA_NOPE:MLA_NOPE + MLA_HALF].reshape(MLA_Q_RANK, -1)
    x2 = w3[:, :, MLA_NOPE + MLA_HALF:].reshape(MLA_Q_RANK, -1)
    packed = jnp.concatenate([nope, x1, x2], axis=1)
    return jnp.pad(packed, ((0, 2 * LANES - MLA_Q_RANK), (0, 0))).astype(BF16)


def _pack_w_ukv(w):
    w3 = w.reshape(MLA_KV_RANK, MLA_HEADS, MLA_NOPE + MLA_V)
    return jnp.concatenate([w3[:, :, :MLA_NOPE].reshape(MLA_KV_RANK, -1),
                            w3[:, :, MLA_NOPE:].reshape(MLA_KV_RANK, -1)], axis=1).astype(BF16)


def _pack_gates(w_r, w_i):
    def block_diag(w):
        n = w.shape[0]
        eye = jnp.eye(n, dtype=w.dtype)
        return jnp.einsum("ncd,nm->ncmd", w, eye).reshape(n * LRU_BLOCK, n * LRU_BLOCK)
    hb = LRU_BLOCKS // 2
    return jnp.stack([jnp.concatenate([block_diag(w_r[j * hb:(j + 1) * hb]),
                                       block_diag(w_i[j * hb:(j + 1) * hb])], axis=1)
                      for j in range(2)]).astype(BF16)


def _chunk_cols(a):
    r = a.shape[0]
    return a.reshape(r, N_FF_CHUNKS, FF_CHUNK).transpose(1, 0, 2)


def kernel(x, p, positions, g_mix, w_in, g_qc, w_uq, g_kvc, w_ukv, b_f, lru_conv_w, lru_conv_b, w_r, b_r, w_i, b_i, lru_lambda, g_out, w_o, g_ffn, w_up, ffn_conv_w, ffn_conv_b, w_down, g_ple, w_ple_gate, w_ple_proj, g_final):
    batch, seq, _ = x.shape
    depth = w_in.shape[0]
    t = batch * seq
    row = lambda v: v.reshape(1, -1)

    cos, sin = _rope_tables(positions)
    h = x.reshape(t, D_MODEL)
    for l in range(depth):
        gqc = jnp.pad(g_qc[l], (0, 2 * LANES - MLA_Q_RANK))
        qm, km, vm, fq, fk, fv, fl, lx, lg = _proj(
            h, row(g_mix[l]), _pack_w_in(w_in[l]), row(gqc), _pack_w_uq(w_uq[l]), row(g_kvc[l]),
            _pack_w_ukv(w_ukv[l]), cos, sin)

        bf_row = jnp.pad(b_f[l], (FLOGIT_LANE, LANES - FLOGIT_LANE - FOX_HEADS))
        g_mla, g_fox, g_lru = (g_out[l, :2 * LANES], g_out[l, 2 * LANES:4 * LANES], g_out[l, 4 * LANES:])
        cq, ck, o_lru = _scan(fl, lx, lg, row(bf_row), lru_conv_w[l], row(lru_conv_b[l]),
                              _pack_gates(w_r[l], w_i[l]), row(b_r[l]), row(b_i[l]), row(lru_lambda[l]),
                              row(g_lru), batch, seq)

        o_mla = _attention(_mla_attn_kernel, "mla_attn", qm, km, vm, [], [], row(g_mla), batch, seq)
        o_fox = _attention(_fox_attn_kernel, "fox_attn", fq, fk, fv, [cq], [ck], row(g_fox), batch, seq)

        h1, xn2 = _oproj(o_mla, o_fox, o_lru, h, w_o[l].astype(BF16), row(g_ffn[l]))

        wup = w_up[l]
        cw, cb = ffn_conv_w[l], row(ffn_conv_b[l])
        h = _ffn(xn2, h1, p[l].reshape(t, PLE_DIM),
                 _chunk_cols(wup[:, :D_FF]).astype(BF16), _chunk_cols(wup[:, D_FF:]).astype(BF16),
                 _chunk_cols(cw[:, :D_FF]), _chunk_cols(cw[:, D_FF:]),
                 _chunk_cols(cb[:, :D_FF]), _chunk_cols(cb[:, D_FF:]),
                 w_down[l].reshape(N_FF_CHUNKS, FF_CHUNK, D_MODEL).astype(BF16),
                 row(g_ple[l]), w_ple_gate[l].astype(BF16), w_ple_proj[l].astype(BF16), row(g_final),
                 batch, seq, final=(l == depth - 1))
    return h.reshape(batch, seq, D_MODEL)
```

```python
import functools
import math

import jax
import jax.numpy as jnp
from jax import lax
from jax.experimental import pallas as pl
from jax.experimental.pallas import tpu as pltpu

F32 = jnp.float32
BF16 = jnp.bfloat16

D_MODEL = 1024
PLE_DIM = 256
MLA_HEADS = 4
MLA_NOPE = 64
MLA_ROPE = 32
MLA_HALF = MLA_ROPE // 2
MLA_V = 64
MLA_Q_RANK = 192
MLA_KV_RANK = 128
FOX_HEADS = 4
FOX_HEAD_DIM = 64
FOX_OUT = FOX_HEADS * FOX_HEAD_DIM
LRU_WIDTH = 512
LRU_BLOCKS = 8
LRU_BLOCK = LRU_WIDTH // LRU_BLOCKS
LRU_CONV = 4
LRU_C = 8.0
D_FF = 2816
FFN_CONV = 3
ROPE_THETA = 10000.0
EPS = 1e-6
LOG2E = math.log2(math.e)

LANES = 128
SUBLANES = 8
V7X_VMEM_BYTES = 64 << 20

ROW_TILE = 512
ATT_BLOCK = 512
FF_CHUNK = 256
N_FF_CHUNKS = D_FF // FF_CHUNK
FFN_ROW_BLOCK = 64

ZC_G0 = 0
ZC_KV = 256
ZC_FOX = 512
ZC_LRU = 1280
ZC_END = 2304
FLOGIT_LANE = MLA_Q_RANK - LANES

MLA_QSCALE = (MLA_NOPE + MLA_ROPE) ** -0.5 * LOG2E
FOX_QSCALE = FOX_HEAD_DIM ** -0.5 * LOG2E


def _vmem_limit(nbytes):
    return int(min(nbytes + (8 << 20), V7X_VMEM_BYTES - (4 << 20)))


def _const_spec(shape):
    nd = len(shape)
    return pl.BlockSpec(shape, lambda *_: (0,) * nd, pipeline_mode=pl.Buffered(1))


def _rms(x, gain, n=None):
    n = x.shape[-1] if n is None else n
    return x * lax.rsqrt(jnp.sum(x * x, axis=-1, keepdims=True) * (1.0 / n) + EPS) * gain


def _dot(a, b):
    return jnp.dot(a, b, preferred_element_type=F32)


def _softplus(x):
    return jnp.maximum(x, 0.0) + jnp.log1p(jnp.exp(-jnp.abs(x)))


def _rope_table_kernel(pos_ref, freq_ref, sign_ref, cos_ref, sin_ref):
    ang = pos_ref[...].astype(F32) * freq_ref[...]
    cos_ref[...] = jnp.cos(ang)
    sin_ref[...] = jnp.sin(ang) * sign_ref[...]


def _rope_tables(positions):
    t = positions.size
    half = MLA_HALF
    freqs = ROPE_THETA ** (-jnp.arange(half, dtype=F32) / half)
    freq_row = jnp.tile(freqs, LANES // half)[None, :]
    sign_row = jnp.where(jnp.arange(LANES) < LANES // 2, -1.0, 1.0).astype(F32)[None, :]
    tm = ROW_TILE
    return pl.pallas_call(
        _rope_table_kernel,
        grid=(t // tm,),
        in_specs=[pl.BlockSpec((tm, 1), lambda i: (i, 0)),
                  _const_spec((1, LANES)), _const_spec((1, LANES))],
        out_specs=[pl.BlockSpec((tm, LANES), lambda i: (i, 0))] * 2,
        out_shape=[jax.ShapeDtypeStruct((t, LANES), F32)] * 2,
        name="rope_tables",
    )(positions.reshape(t, 1), freq_row, sign_row)


def _store_value_slots(ref, v_pairs):
    half = LANES // 2
    lane = lax.broadcasted_iota(jnp.int32, (v_pairs.shape[0], LANES), 1)
    for h in range(v_pairs.shape[1] // half):
        p, e = divmod(h, 2)
        own = (lane >= half * e) & (lane < half * (e + 1))
        ref[:, LANES * h:LANES * (h + 1)] = jnp.where(own, v_pairs[:, LANES * p:LANES * (p + 1)], 1.0).astype(BF16)


def _proj_kernel(h_ref, gmix_ref, win_ref, gqc_ref, wq_ref, gkvc_ref, wkv_ref, cos_ref, sin_ref,
                 qm_ref, km_ref, vm_ref, fq_ref, fk_ref, fv_ref, fl_ref, lx_ref, lg_ref):
    xn = _rms(h_ref[...], gmix_ref[...]).astype(BF16)
    cos = cos_ref[...]
    sin = sin_ref[...]

    def zcols(lo, hi):
        return _dot(xn, win_ref[:, lo:hi])

    def rotate(r):
        return r * cos + pltpu.roll(r, LANES // 2, axis=1) * sin

    g0 = zcols(ZC_G0, ZC_KV)
    lane = lax.broadcasted_iota(jnp.int32, g0.shape, 1)
    qc = jnp.where(lane < MLA_Q_RANK, g0, 0.0)
    qcn = _rms(qc, gqc_ref[...], MLA_Q_RANK).astype(BF16)
    qq = _dot(qcn, wq_ref[...]) * MLA_QSCALE
    q_rot = rotate(qq[:, 2 * LANES:]).astype(BF16)
    qm_ref[:, 0:LANES] = qq[:, 0:LANES].astype(BF16)
    qm_ref[:, LANES:2 * LANES] = q_rot
    qm_ref[:, 2 * LANES:3 * LANES] = qq[:, LANES:2 * LANES].astype(BF16)
    qm_ref[:, 3 * LANES:] = q_rot
    fl_ref[...] = g0[:, LANES:]

    kvz = zcols(ZC_KV, ZC_FOX)
    kvn = _rms(kvz[:, :MLA_KV_RANK], gkvc_ref[...]).astype(BF16)
    kv = _dot(kvn, wkv_ref[...])
    k_rot = rotate(kvz[:, MLA_KV_RANK:]).astype(BF16)
    km_ref[:, 0:LANES] = kv[:, 0:LANES].astype(BF16)
    km_ref[:, LANES:2 * LANES] = k_rot
    km_ref[:, 2 * LANES:3 * LANES] = kv[:, LANES:2 * LANES].astype(BF16)
    km_ref[:, 3 * LANES:] = k_rot
    _store_value_slots(vm_ref, kv[:, 2 * LANES:])

    f = zcols(ZC_FOX, ZC_LRU)
    fq_ref[...] = (f[:, :FOX_OUT] * FOX_QSCALE).astype(BF16)
    fk_ref[...] = f[:, FOX_OUT:2 * FOX_OUT].astype(BF16)
    _store_value_slots(fv_ref, f[:, 2 * FOX_OUT:])

    lz = zcols(ZC_LRU, ZC_END)
    lx_ref[...] = lz[:, :LRU_WIDTH]
    lg_ref[...] = lz[:, LRU_WIDTH:]


def _proj(h, gmix, win, gqc, wq, gkvc, wkv, cos, sin):
    t = h.shape[0]
    tm = ROW_TILE
    row = lambda c: pl.BlockSpec((tm, c), lambda i: (i, 0))
    outs = [(4 * LANES, BF16), (4 * LANES, BF16), (MLA_HEADS * LANES, BF16),
            (FOX_OUT, BF16), (FOX_OUT, BF16), (FOX_HEADS * LANES, BF16),
            (LANES, F32), (LRU_WIDTH, F32), (LRU_WIDTH, F32)]
    weights = (gmix, win, gqc, wq, gkvc, wkv)
    return pl.pallas_call(
        _proj_kernel,
        grid=(t // tm,),
        in_specs=[row(D_MODEL)] + [_const_spec(w.shape) for w in weights] + [row(LANES), row(LANES)],
        out_specs=[row(c) for c, _ in outs],
        out_shape=[jax.ShapeDtypeStruct((t, c), d) for c, d in outs],
        compiler_params=pltpu.CompilerParams(
            dimension_semantics=("parallel",), vmem_limit_bytes=_vmem_limit(40 << 20)),
        name="proj",
    )(h, *weights, cos, sin)


def _scan_kernel(fl_ref, lx_ref, lg_ref, bf_ref, cw_ref, cb_ref, wri_ref, br_ref, bi_ref, lam_ref,
                 g_ref, cq_ref, ck_ref, o_ref, xbuf, hcar, ccar):
    tm = lx_ref.shape[0]
    halo = SUBLANES

    @pl.when(pl.program_id(1) == 0)
    def _():
        xbuf[0:halo, :] = jnp.zeros((halo, LRU_WIDTH), F32)
        hcar[...] = jnp.zeros_like(hcar)
        ccar[...] = jnp.zeros_like(ccar)

    lane = lax.broadcasted_iota(jnp.int32, (tm, LANES), 1)
    row = lax.broadcasted_iota(jnp.int32, (tm, LANES), 0)
    valid = (lane >= FLOGIT_LANE) & (lane < FLOGIT_LANE + FOX_HEADS)
    c = jnp.where(valid, -_softplus(-(fl_ref[...] + bf_ref[...])), 0.0)
    shift = 1
    while shift < tm:
        c = c + jnp.where(row >= shift, pltpu.roll(c, shift, axis=0), 0.0)
        shift *= 2
    c = c + ccar[...]
    ccar[...] = c[tm - 1:tm, :]
    c2 = c * LOG2E
    cq_ref[...] = c2
    ck_ref[...] = c2.T[FLOGIT_LANE:FLOGIT_LANE + SUBLANES, :]

    xbuf[halo:halo + tm, :] = lx_ref[...]
    xc = cb_ref[...] + cw_ref[LRU_CONV - 1:LRU_CONV, :] * lx_ref[...]
    for k in range(LRU_CONV - 1):
        back = LRU_CONV - 1 - k
        xc = xc + cw_ref[k:k + 1, :] * xbuf[halo - back:halo - back + tm, :]
    xbuf[0:halo, :] = xbuf[tm:tm + halo, :]

    xcb = xc.astype(BF16)
    half = LRU_WIDTH // 2
    gates = [_dot(xcb[:, j * half:(j + 1) * half], wri_ref[j]) for j in range(2)]
    r = jax.nn.sigmoid(jnp.concatenate([g[:, :half] for g in gates], axis=1) + br_ref[...])
    i = jax.nn.sigmoid(jnp.concatenate([g[:, half:] for g in gates], axis=1) + bi_ref[...])
    log_a = (-LRU_C) * r * _softplus(-lam_ref[...])
    a = jnp.exp(log_a)
    b = jnp.sqrt(-jnp.tanh(log_a) * (a * a + 1.0)) * (i * xc)

    rowl = lax.broadcasted_iota(jnp.int32, (tm, LRU_WIDTH), 0)
    shift = 1
    while shift < tm:
        keep = rowl >= shift
        a_sh = jnp.where(keep, pltpu.roll(a, shift, axis=0), 1.0)
        b_sh = jnp.where(keep, pltpu.roll(b, shift, axis=0), 0.0)
        b = a * b_sh + b
        a = a * a_sh
        shift *= 2
    h = b + a * hcar[...]
    hcar[...] = h[tm - 1:tm, :]

    o = h * jax.nn.gelu(lg_ref[...], approximate=True)
    o_ref[...] = _rms(o, g_ref[...]).astype(BF16)


def _scan(fl, lx, lg, bf_row, cw, cb, wri, br, bi, lam, g_lru, batch, seq):
    tm = ATT_BLOCK
    ns = seq // tm
    rows = lambda c: pl.BlockSpec((tm, c), lambda b, s: (b * ns + s, 0))
    weights = (bf_row, cw, cb, wri, br, bi, lam, g_lru)
    t = batch * seq
    return pl.pallas_call(
        _scan_kernel,
        grid=(batch, ns),
        in_specs=[rows(LANES), rows(LRU_WIDTH), rows(LRU_WIDTH)] + [_const_spec(w.shape) for w in weights],
        out_specs=[rows(LANES),
                   pl.BlockSpec((None, None, SUBLANES, tm), lambda b, s: (b, s, 0, 0)),
                   rows(LRU_WIDTH)],
        out_shape=[jax.ShapeDtypeStruct((t, LANES), F32),
                   jax.ShapeDtypeStruct((batch, ns, SUBLANES, tm), F32),
                   jax.ShapeDtypeStruct((t, LRU_WIDTH), BF16)],
        scratch_shapes=[pltpu.VMEM((tm + SUBLANES, LRU_WIDTH), F32),
                        pltpu.VMEM((1, LRU_WIDTH), F32),
                        pltpu.VMEM((1, LANES), F32)],
        compiler_params=pltpu.CompilerParams(
            dimension_semantics=("parallel", "arbitrary"), vmem_limit_bytes=_vmem_limit(32 << 20)),
        name="scan",
    )(fl, lx, lg, *weights)


def _flash_heads(q_sc, load_k, load_v, biases, qi, blk, s_sc, p_sc, m_sc, alpha_sc, acc_sc):
    nh = q_sc.shape[0]
    m_sc[...] = jnp.full(m_sc.shape, -jnp.inf, F32)
    acc_sc[...] = jnp.zeros(acc_sc.shape, F32)
    row = lax.broadcasted_iota(jnp.int32, (blk, blk), 0)
    col = lax.broadcasted_iota(jnp.int32, (blk, blk), 1)

    def scores(h, j, diagonal):
        s = lax.dot_general(q_sc[h], load_k(h, j), (((1,), (1,)), ((), ())), preferred_element_type=F32)
        if biases is not None:
            bias_q, load_bias_k = biases[h]
            s = s + bias_q - load_bias_k(j)
        if diagonal:
            s = jnp.where(col <= row, s, -jnp.inf)
        s_sc[h] = s

    def softmax(h):
        m_old = m_sc[h]
        m_blk = jnp.max(s_sc[h], axis=-1, keepdims=True)
        m_new = jnp.maximum(m_old, jnp.broadcast_to(m_blk, (blk, LANES)))
        m_sc[h] = m_new
        alpha_sc[h] = jnp.exp2(m_old - m_new)
        for c in range(0, blk, LANES):
            p_sc[h, :, c:c + LANES] = jnp.exp2(s_sc[h, :, c:c + LANES] - m_new).astype(BF16)

    def weighted_values(h, j):
        acc_sc[h] = alpha_sc[h] * acc_sc[h] + _dot(p_sc[h], load_v(h, j))

    def block(j, diagonal):
        for h in range(nh):
            scores(h, j, diagonal)
        for h in range(nh):
            softmax(h)
            weighted_values(h, j)

    def body(j, carry):
        block(j, False)
        return carry

    lax.fori_loop(0, qi, body, 0)
    block(qi, True)
    return [acc_sc[h] for h in range(nh)]


def _pair_norm_store(o_ref, accs, g_ref):
    half = LANES // 2
    lane = lax.broadcasted_iota(jnp.int32, accs[0].shape, 1)
    heads = [a / pltpu.roll(a, half, axis=1) for a in accs]
    pairs = [jnp.where(lane < half, heads[i], heads[i + 1]) for i in range(0, len(heads), 2)]
    n = len(pairs) * LANES
    ssq = sum(jnp.sum(p * p, axis=-1, keepdims=True) for p in pairs)
    rs = lax.rsqrt(ssq * (1.0 / n) + EPS)
    for i, p in enumerate(pairs):
        o_ref[:, i * LANES:(i + 1) * LANES] = (p * rs * g_ref[:, i * LANES:(i + 1) * LANES]).astype(BF16)


def _kv_rows(j, blk):
    return pl.ds(pl.multiple_of(j * blk, blk), blk)


def _mla_attn_kernel(q_ref, k_ref, v_ref, g_ref, o_ref, q_sc, *scratch):
    blk = q_ref.shape[0]
    lane = lax.broadcasted_iota(jnp.int32, (blk, 2 * LANES), 1)
    for h in range(MLA_HEADS):
        p, e = divmod(h, 2)
        x1 = LANES + MLA_HALF * h
        x2 = x1 + LANES // 2
        mine = (((lane >= MLA_NOPE * e) & (lane < MLA_NOPE * (e + 1)))
                | ((lane >= x1) & (lane < x1 + MLA_HALF))
                | ((lane >= x2) & (lane < x2 + MLA_HALF)))
        q_pair = q_ref[:, 2 * LANES * p:2 * LANES * (p + 1)]
        q_sc[h] = jnp.where(mine, q_pair, jnp.zeros_like(q_pair))
    load_k = lambda h, j: k_ref[_kv_rows(j, blk), 2 * LANES * (h // 2):2 * LANES * (h // 2 + 1)]
    load_v = lambda h, j: v_ref[_kv_rows(j, blk), LANES * h:LANES * (h + 1)]
    accs = _flash_heads(q_sc, load_k, load_v, None, pl.program_id(1), blk, *scratch)
    _pair_norm_store(o_ref, accs, g_ref)


def _fox_attn_kernel(q_ref, k_ref, v_ref, cq_ref, ck_ref, g_ref, o_ref, q_sc, *scratch):
    blk = q_ref.shape[0]
    lane = lax.broadcasted_iota(jnp.int32, (blk, LANES), 1)
    biases = []
    for h in range(FOX_HEADS):
        p, e = divmod(h, 2)
        mine = (lane >= FOX_HEAD_DIM * e) & (lane < FOX_HEAD_DIM * (e + 1))
        q_pair = q_ref[:, LANES * p:LANES * (p + 1)]
        q_sc[h] = jnp.where(mine, q_pair, jnp.zeros_like(q_pair))
        biases.append((cq_ref[:, FLOGIT_LANE + h:FLOGIT_LANE + h + 1],
                       lambda j, h=h: ck_ref[j, h:h + 1, :]))
    load_k = lambda h, j: k_ref[_kv_rows(j, blk), LANES * (h // 2):LANES * (h // 2 + 1)]
    load_v = lambda h, j: v_ref[_kv_rows(j, blk), LANES * h:LANES * (h + 1)]
    accs = _flash_heads(q_sc, load_k, load_v, biases, pl.program_id(1), blk, *scratch)
    _pair_norm_store(o_ref, accs, g_ref)


def _attention(kernel, name, q, k, v, extra_tiles, extra_full, gain, batch, seq, heads, dk):
    blk = ATT_BLOCK
    nq = seq // blk
    tile = lambda a: pl.BlockSpec((None, blk, a.shape[-1]), lambda b, i: (b, i, 0))
    full = lambda a: pl.BlockSpec((None,) + a.shape[1:], lambda b, i: (b,) + (0,) * (a.ndim - 1))
    q3, k3, v3 = (a.reshape(batch, seq, a.shape[-1]) for a in (q, k, v))
    extra_tiles = [a.reshape(batch, seq, a.shape[-1]) for a in extra_tiles]
    d_out = v.shape[-1] // 2
    out = pl.pallas_call(
        kernel,
        grid=(batch, nq),
        in_specs=([tile(q3), full(k3), full(v3)] + [tile(a) for a in extra_tiles]
                  + [full(a) for a in extra_full] + [_const_spec(gain.shape)]),
        out_specs=pl.BlockSpec((None, blk, d_out), lambda b, i: (b, i, 0)),
        out_shape=jax.ShapeDtypeStruct((batch, seq, d_out), BF16),
        scratch_shapes=[pltpu.VMEM((heads, blk, dk), BF16),
                        pltpu.VMEM((heads, blk, blk), F32),
                        pltpu.VMEM((heads, blk, blk), BF16),
                        pltpu.VMEM((heads, blk, LANES), F32),
                        pltpu.VMEM((heads, blk, LANES), F32),
                        pltpu.VMEM((heads, blk, LANES), F32)],
        compiler_params=pltpu.CompilerParams(
            dimension_semantics=("parallel", "arbitrary"), vmem_limit_bytes=_vmem_limit(40 << 20)),
        name=name,
    )(q3, k3, v3, *extra_tiles, *extra_full, gain)
    return out.reshape(batch * seq, d_out)


def _oproj_kernel(om_ref, of_ref, ol_ref, h_ref, wo_ref, g_ref, h1_ref, xn_ref):
    a, b = om_ref.shape[1], om_ref.shape[1] + of_ref.shape[1]
    y = (_dot(om_ref[...], wo_ref[0:a, :]) + _dot(of_ref[...], wo_ref[a:b, :])
         + _dot(ol_ref[...], wo_ref[b:, :]))
    h1 = h_ref[...] + y
    h1_ref[...] = h1
    xn_ref[...] = _rms(h1, g_ref[...]).astype(BF16)


def _oproj(om, of, ol, h, wo, g_ffn):
    t = h.shape[0]
    tm = ROW_TILE
    row = lambda c: pl.BlockSpec((tm, c), lambda i: (i, 0))
    return pl.pallas_call(
        _oproj_kernel,
        grid=(t // tm,),
        in_specs=[row(om.shape[1]), row(of.shape[1]), row(ol.shape[1]), row(D_MODEL),
                  _const_spec(wo.shape), _const_spec(g_ffn.shape)],
        out_specs=[row(D_MODEL), row(D_MODEL)],
        out_shape=[jax.ShapeDtypeStruct((t, D_MODEL), F32), jax.ShapeDtypeStruct((t, D_MODEL), BF16)],
        compiler_params=pltpu.CompilerParams(
            dimension_semantics=("parallel",), vmem_limit_bytes=_vmem_limit(24 << 20)),
        name="oproj",
    )(om, of, ol, h, wo, g_ffn)


def _ffn_kernel(xn_ref, h1_ref, p_ref, wg_ref, wv_ref, cwg_ref, cwv_ref, cbg_ref, cbv_ref, wd_ref,
                gple_ref, wgate_ref, wproj_ref, gfin_ref, out_ref,
                ubuf_g, ubuf_v, tail_g, tail_v, acc_ref, act_buf, *, final):
    tm = xn_ref.shape[0]
    halo = SUBLANES

    @pl.when(pl.program_id(1) == 0)
    def _():
        tail_g[...] = jnp.zeros_like(tail_g)
        tail_v[...] = jnp.zeros_like(tail_v)

    acc_ref[...] = jnp.zeros_like(acc_ref)
    xn = xn_ref[...]

    def up(c):
        ubuf_g[c % 2, halo:halo + tm, :] = _dot(xn, wg_ref[c])
        ubuf_v[c % 2, halo:halo + tm, :] = _dot(xn, wv_ref[c])

    def conv(buf, r0, cw, cb):
        taps = [cw[k:k + 1, :] * buf[halo + r0 - (FFN_CONV - 1 - k):halo + r0 - (FFN_CONV - 1 - k) + FFN_ROW_BLOCK, :]
                for k in range(FFN_CONV)]
        return cb + taps[0] + taps[1] + taps[2]

    def gate(c):
        bg, bv = ubuf_g.at[c % 2], ubuf_v.at[c % 2]
        bg[0:halo, :] = tail_g[c]
        bv[0:halo, :] = tail_v[c]
        cwg, cwv, cbg, cbv = cwg_ref[c], cwv_ref[c], cbg_ref[c], cbv_ref[c]
        for r0 in range(0, tm, FFN_ROW_BLOCK):
            g = conv(bg, r0, cwg, cbg)
            v = conv(bv, r0, cwv, cbv)
            act_buf[c % 2, r0:r0 + FFN_ROW_BLOCK, :] = (g * jax.nn.sigmoid(g) * v).astype(BF16)
        tail_g[c] = bg[tm:tm + halo, :]
        tail_v[c] = bv[tm:tm + halo, :]

    def down(c):
        acc_ref[...] += _dot(act_buf[c % 2], wd_ref[c])

    up(0)
    for c in range(N_FF_CHUNKS):
        if c + 1 < N_FF_CHUNKS:
            up(c + 1)
        if c >= 1:
            down(c - 1)
        gate(c)
    down(N_FF_CHUNKS - 1)

    h2 = h1_ref[...] + acc_ref[...]
    gate = jax.nn.sigmoid(_dot(_rms(h2, gple_ref[...]).astype(BF16), wgate_ref[...]))
    h3 = h2 + gate * _dot(p_ref[...].astype(BF16), wproj_ref[...])
    out_ref[...] = _rms(h3, gfin_ref[...]) if final else h3


def _ffn(xn, h1, p_l, wg, wv, cwg, cwv, cbg, cbv, wd, gple, wgate, wproj, gfin, batch, seq, final):
    tm = ROW_TILE
    ns = seq // tm
    t = batch * seq
    rows = lambda c: pl.BlockSpec((tm, c), lambda b, s: (b * ns + s, 0))
    weights = (wg, wv, cwg, cwv, cbg, cbv, wd, gple, wgate, wproj, gfin)
    return pl.pallas_call(
        functools.partial(_ffn_kernel, final=final),
        grid=(batch, ns),
        in_specs=[rows(D_MODEL), rows(D_MODEL), rows(PLE_DIM)] + [_const_spec(w.shape) for w in weights],
        out_specs=rows(D_MODEL),
        out_shape=jax.ShapeDtypeStruct((t, D_MODEL), F32),
        scratch_shapes=[pltpu.VMEM((2, tm + SUBLANES, FF_CHUNK), F32),
                        pltpu.VMEM((2, tm + SUBLANES, FF_CHUNK), F32),
                        pltpu.VMEM((N_FF_CHUNKS, SUBLANES, FF_CHUNK), F32),
                        pltpu.VMEM((N_FF_CHUNKS, SUBLANES, FF_CHUNK), F32),
                        pltpu.VMEM((tm, D_MODEL), F32),
                        pltpu.VMEM((2, tm, FF_CHUNK), BF16)],
        compiler_params=pltpu.CompilerParams(
            dimension_semantics=("parallel", "arbitrary"), vmem_limit_bytes=_vmem_limit(48 << 20)),
        name="ffn_ple",
    )(xn, h1, p_l, *weights)


def _pack_w_in(w):
    o_kvc = MLA_Q_RANK
    o_kr = o_kvc + MLA_KV_RANK
    o_fox = o_kr + MLA_ROPE
    o_fl = o_fox + 3 * FOX_OUT
    o_lru = o_fl + FOX_HEADS
    pad = jnp.zeros((w.shape[0], ZC_KV - MLA_Q_RANK - FOX_HEADS), w.dtype)
    kr1 = jnp.tile(w[:, o_kr:o_kr + MLA_HALF], (1, MLA_HEADS))
    kr2 = jnp.tile(w[:, o_kr + MLA_HALF:o_fox], (1, MLA_HEADS))
    return jnp.concatenate(
        [w[:, :MLA_Q_RANK], w[:, o_fl:o_lru], pad, w[:, o_kvc:o_kr], kr1, kr2, w[:, o_fox:o_fl], w[:, o_lru:]],
        axis=1).astype(BF16)


def _pack_w_uq(w):
    w3 = w.reshape(MLA_Q_RANK, MLA_HEADS, MLA_NOPE + MLA_ROPE)
    nope = w3[:, :, :MLA_NOPE].reshape(MLA_Q_RANK, -1)
    x1 = w3[:, :, MLA_NOPE:MLA_NOPE + MLA_HALF].reshape(MLA_Q_RANK, -1)
    x2 = w3[:, :, MLA_NOPE + MLA_HALF:].reshape(MLA_Q_RANK, -1)
    packed = jnp.concatenate([nope, x1, x2], axis=1)
    return jnp.pad(packed, ((0, 2 * LANES - MLA_Q_RANK), (0, 0))).astype(BF16)


def _pack_w_ukv(w):
    w3 = w.reshape(MLA_KV_RANK, MLA_HEADS, MLA_NOPE + MLA_V)
    return jnp.concatenate([w3[:, :, :MLA_NOPE].reshape(MLA_KV_RANK, -1),
                            w3[:, :, MLA_NOPE:].reshape(MLA_KV_RANK, -1)], axis=1).astype(BF16)


def _pack_gates(w_r, w_i):
    def block_diag(w):
        n = w.shape[0]
        eye = jnp.eye(n, dtype=w.dtype)
        return jnp.einsum("ncd,nm->ncmd", w, eye).reshape(n * LRU_BLOCK, n * LRU_BLOCK)
    hb = LRU_BLOCKS // 2
    return jnp.stack([jnp.concatenate([block_diag(w_r[j * hb:(j + 1) * hb]),
                                       block_diag(w_i[j * hb:(j + 1) * hb])], axis=1)
                      for j in range(2)]).astype(BF16)


def _chunk_cols(a):
    r = a.shape[0]
    return a.reshape(r, N_FF_CHUNKS, FF_CHUNK).transpose(1, 0, 2)


def kernel(x, p, positions, g_mix, w_in, g_qc, w_uq, g_kvc, w_ukv, b_f, lru_conv_w, lru_conv_b, w_r, b_r, w_i, b_i, lru_lambda, g_out, w_o, g_ffn, w_up, ffn_conv_w, ffn_conv_b, w_down, g_ple, w_ple_gate, w_ple_proj, g_final):
    batch, seq, _ = x.shape
    depth = w_in.shape[0]
    t = batch * seq
    row = lambda v: v.reshape(1, -1)

    cos, sin = _rope_tables(positions)
    h = x.reshape(t, D_MODEL)
    for l in range(depth):
        gqc = jnp.pad(g_qc[l], (0, 2 * LANES - MLA_Q_RANK))
        qm, km, vm, fq, fk, fv, fl, lx, lg = _proj(
            h, row(g_mix[l]), _pack_w_in(w_in[l]), row(gqc), _pack_w_uq(w_uq[l]), row(g_kvc[l]),
            _pack_w_ukv(w_ukv[l]), cos, sin)

        bf_row = jnp.pad(b_f[l], (FLOGIT_LANE, LANES - FLOGIT_LANE - FOX_HEADS))
        g_mla, g_fox, g_lru = (g_out[l, :2 * LANES], g_out[l, 2 * LANES:4 * LANES], g_out[l, 4 * LANES:])
        cq, ck, o_lru = _scan(fl, lx, lg, row(bf_row), lru_conv_w[l], row(lru_conv_b[l]),
                              _pack_gates(w_r[l], w_i[l]), row(b_r[l]), row(b_i[l]), row(lru_lambda[l]),
                              row(g_lru), batch, seq)

        o_mla = _attention(_mla_attn_kernel, "mla_attn", qm, km, vm, [], [], row(g_mla), batch, seq,
                           MLA_HEADS, 2 * LANES)
        o_fox = _attention(_fox_attn_kernel, "fox_attn", fq, fk, fv, [cq], [ck], row(g_fox), batch, seq,
                           FOX_HEADS, LANES)

        h1, xn2 = _oproj(o_mla, o_fox, o_lru, h, w_o[l].astype(BF16), row(g_ffn[l]))

        wup = w_up[l]
        cw, cb = ffn_conv_w[l], row(ffn_conv_b[l])
        h = _ffn(xn2, h1, p[l].reshape(t, PLE_DIM),
                 _chunk_cols(wup[:, :D_FF]).astype(BF16), _chunk_cols(wup[:, D_FF:]).astype(BF16),
                 _chunk_cols(cw[:, :D_FF]), _chunk_cols(cw[:, D_FF:]),
                 _chunk_cols(cb[:, :D_FF]), _chunk_cols(cb[:, D_FF:]),
                 w_down[l].reshape(N_FF_CHUNKS, FF_CHUNK, D_MODEL).astype(BF16),
                 row(g_ple[l]), w_ple_gate[l].astype(BF16), w_ple_proj[l].astype(BF16), row(g_final),
                 batch, seq, final=(l == depth - 1))
    return h.reshape(batch, seq, D_MODEL)
```

```python
import functools
import math

import jax
import jax.numpy as jnp
from jax import lax
from jax.experimental import pallas as pl
from jax.experimental.pallas import tpu as pltpu

F32 = jnp.float32
BF16 = jnp.bfloat16

D_MODEL = 1024
PLE_DIM = 256
MLA_HEADS = 4
MLA_NOPE = 64
MLA_ROPE = 32
MLA_HALF = MLA_ROPE // 2
MLA_V = 64
MLA_Q_RANK = 192
MLA_KV_RANK = 128
FOX_HEADS = 4
FOX_HEAD_DIM = 64
FOX_OUT = FOX_HEADS * FOX_HEAD_DIM
LRU_WIDTH = 512
LRU_BLOCKS = 8
LRU_BLOCK = LRU_WIDTH // LRU_BLOCKS
LRU_CONV = 4
LRU_C = 8.0
D_FF = 2816
FFN_CONV = 3
ROPE_THETA = 10000.0
EPS = 1e-6
LOG2E = math.log2(math.e)

LANES = 128
SUBLANES = 8
V7X_VMEM_BYTES = 64 << 20

ROW_TILE = 512
ATT_BLOCK = 512
FF_CHUNK = 256
N_FF_CHUNKS = D_FF // FF_CHUNK
FFN_ROW_BLOCK = 64

ZC_G0 = 0
ZC_KV = 256
ZC_FOX = 512
ZC_LRU = 1280
ZC_END = 2304
FLOGIT_LANE = MLA_Q_RANK - LANES

MLA_QSCALE = (MLA_NOPE + MLA_ROPE) ** -0.5 * LOG2E
FOX_QSCALE = FOX_HEAD_DIM ** -0.5 * LOG2E


def _vmem_limit(nbytes):
    return int(min(nbytes + (8 << 20), V7X_VMEM_BYTES - (4 << 20)))


def _const_spec(shape):
    nd = len(shape)
    return pl.BlockSpec(shape, lambda *_: (0,) * nd, pipeline_mode=pl.Buffered(1))


def _rms(x, gain, n=None):
    n = x.shape[-1] if n is None else n
    return x * lax.rsqrt(jnp.sum(x * x, axis=-1, keepdims=True) * (1.0 / n) + EPS) * gain


def _dot(a, b):
    return jnp.dot(a, b, preferred_element_type=F32)


def _softplus(x):
    return jnp.maximum(x, 0.0) + jnp.log1p(jnp.exp(-jnp.abs(x)))


def _rope_table_kernel(pos_ref, freq_ref, sign_ref, cos_ref, sin_ref):
    ang = pos_ref[...].astype(F32) * freq_ref[...]
    cos_ref[...] = jnp.cos(ang)
    sin_ref[...] = jnp.sin(ang) * sign_ref[...]


def _rope_tables(positions):
    t = positions.size
    half = MLA_HALF
    freqs = ROPE_THETA ** (-jnp.arange(half, dtype=F32) / half)
    freq_row = jnp.tile(freqs, LANES // half)[None, :]
    sign_row = jnp.where(jnp.arange(LANES) < LANES // 2, -1.0, 1.0).astype(F32)[None, :]
    tm = ROW_TILE
    return pl.pallas_call(
        _rope_table_kernel,
        grid=(t // tm,),
        in_specs=[pl.BlockSpec((tm, 1), lambda i: (i, 0)),
                  _const_spec((1, LANES)), _const_spec((1, LANES))],
        out_specs=[pl.BlockSpec((tm, LANES), lambda i: (i, 0))] * 2,
        out_shape=[jax.ShapeDtypeStruct((t, LANES), F32)] * 2,
        name="rope_tables",
    )(positions.reshape(t, 1), freq_row, sign_row)


def _store_value_slots(ref, v_pairs):
    half = LANES // 2
    lane = lax.broadcasted_iota(jnp.int32, (v_pairs.shape[0], LANES), 1)
    for h in range(v_pairs.shape[1] // half):
        p, e = divmod(h, 2)
        own = (lane >= half * e) & (lane < half * (e + 1))
        ref[:, LANES * h:LANES * (h + 1)] = jnp.where(own, v_pairs[:, LANES * p:LANES * (p + 1)], 1.0).astype(BF16)


def _proj_kernel(h_ref, gmix_ref, win_ref, gqc_ref, wq_ref, gkvc_ref, wkv_ref, cos_ref, sin_ref,
                 qm_ref, km_ref, vm_ref, fq_ref, fk_ref, fv_ref, fl_ref, lx_ref, lg_ref):
    xn = _rms(h_ref[...], gmix_ref[...]).astype(BF16)
    cos = cos_ref[...]
    sin = sin_ref[...]

    def zcols(lo, hi):
        return _dot(xn, win_ref[:, lo:hi])

    def rotate(r):
        return r * cos + pltpu.roll(r, LANES // 2, axis=1) * sin

    g0 = zcols(ZC_G0, ZC_KV)
    lane = lax.broadcasted_iota(jnp.int32, g0.shape, 1)
    qc = jnp.where(lane < MLA_Q_RANK, g0, 0.0)
    qcn = _rms(qc, gqc_ref[...], MLA_Q_RANK).astype(BF16)
    qq = _dot(qcn, wq_ref[...]) * MLA_QSCALE
    q_rot = rotate(qq[:, 2 * LANES:]).astype(BF16)
    qm_ref[:, 0:LANES] = qq[:, 0:LANES].astype(BF16)
    qm_ref[:, LANES:2 * LANES] = q_rot
    qm_ref[:, 2 * LANES:3 * LANES] = qq[:, LANES:2 * LANES].astype(BF16)
    qm_ref[:, 3 * LANES:] = q_rot
    fl_ref[...] = g0[:, LANES:]

    kvz = zcols(ZC_KV, ZC_FOX)
    kvn = _rms(kvz[:, :MLA_KV_RANK], gkvc_ref[...]).astype(BF16)
    kv = _dot(kvn, wkv_ref[...])
    k_rot = rotate(kvz[:, MLA_KV_RANK:]).astype(BF16)
    km_ref[:, 0:LANES] = kv[:, 0:LANES].astype(BF16)
    km_ref[:, LANES:2 * LANES] = k_rot
    km_ref[:, 2 * LANES:3 * LANES] = kv[:, LANES:2 * LANES].astype(BF16)
    km_ref[:, 3 * LANES:] = k_rot
    _store_value_slots(vm_ref, kv[:, 2 * LANES:])

    f = zcols(ZC_FOX, ZC_LRU)
    fq_ref[...] = (f[:, :FOX_OUT] * FOX_QSCALE).astype(BF16)
    fk_ref[...] = f[:, FOX_OUT:2 * FOX_OUT].astype(BF16)
    _store_value_slots(fv_ref, f[:, 2 * FOX_OUT:])

    lz = zcols(ZC_LRU, ZC_END)
    lx_ref[...] = lz[:, :LRU_WIDTH]
    lg_ref[...] = lz[:, LRU_WIDTH:]


def _proj(h, gmix, win, gqc, wq, gkvc, wkv, cos, sin):
    t = h.shape[0]
    tm = ROW_TILE
    row = lambda c: pl.BlockSpec((tm, c), lambda i: (i, 0))
    outs = [(4 * LANES, BF16), (4 * LANES, BF16), (MLA_HEADS * LANES, BF16),
            (FOX_OUT, BF16), (FOX_OUT, BF16), (FOX_HEADS * LANES, BF16),
            (LANES, F32), (LRU_WIDTH, F32), (LRU_WIDTH, F32)]
    weights = (gmix, win, gqc, wq, gkvc, wkv)
    return pl.pallas_call(
        _proj_kernel,
        grid=(t // tm,),
        in_specs=[row(D_MODEL)] + [_const_spec(w.shape) for w in weights] + [row(LANES), row(LANES)],
        out_specs=[row(c) for c, _ in outs],
        out_shape=[jax.ShapeDtypeStruct((t, c), d) for c, d in outs],
        compiler_params=pltpu.CompilerParams(
            dimension_semantics=("parallel",), vmem_limit_bytes=_vmem_limit(40 << 20)),
        name="proj",
    )(h, *weights, cos, sin)


def _store_decay_slots(c2, fq_ref, fk_ref, fqa_ref, fka_ref):
    tm = c2.shape[0]
    half = LANES // 2
    lane = lax.broadcasted_iota(jnp.int32, (tm, LANES), 1)
    for h in range(FOX_HEADS):
        p, e = divmod(h, 2)
        ch = jnp.broadcast_to(c2[:, FLOGIT_LANE + h:FLOGIT_LANE + h + 1], (tm, LANES))
        hi = ch.astype(BF16).astype(F32)
        rest = ch - hi
        mid = rest.astype(BF16).astype(F32)
        lo = rest - mid
        parts = (hi, mid, lo)
        base = half * (1 - e)
        aug_q = jnp.where((lane >= base + 3) & (lane < base + 6), 1.0, 0.0)
        aug_k = jnp.where((lane >= base) & (lane < base + 3), 1.0, 0.0)
        for i, part in enumerate(parts):
            aug_q = jnp.where(lane == base + i, part, aug_q)
            aug_k = jnp.where(lane == base + 3 + i, -part, aug_k)
        own = (lane >= half * e) & (lane < half * (e + 1))
        cols = slice(LANES * p, LANES * (p + 1))
        slot = slice(LANES * h, LANES * (h + 1))
        fqa_ref[:, slot] = jnp.where(own, fq_ref[:, cols], aug_q.astype(BF16))
        fka_ref[:, slot] = jnp.where(own, fk_ref[:, cols], aug_k.astype(BF16))


def _scan_kernel(fl_ref, fq_ref, fk_ref, lx_ref, lg_ref, bf_ref, cw_ref, cb_ref, wri_ref, br_ref, bi_ref,
                 lam_ref, g_ref, fqa_ref, fka_ref, o_ref, xbuf, hcar, ccar):
    tm = lx_ref.shape[0]
    halo = SUBLANES

    @pl.when(pl.program_id(1) == 0)
    def _():
        xbuf[0:halo, :] = jnp.zeros((halo, LRU_WIDTH), F32)
        hcar[...] = jnp.zeros_like(hcar)
        ccar[...] = jnp.zeros_like(ccar)

    lane = lax.broadcasted_iota(jnp.int32, (tm, LANES), 1)
    row = lax.broadcasted_iota(jnp.int32, (tm, LANES), 0)
    valid = (lane >= FLOGIT_LANE) & (lane < FLOGIT_LANE + FOX_HEADS)
    c = jnp.where(valid, -_softplus(-(fl_ref[...] + bf_ref[...])), 0.0)
    shift = 1
    while shift < tm:
        c = c + jnp.where(row >= shift, pltpu.roll(c, shift, axis=0), 0.0)
        shift *= 2
    c = c + ccar[...]
    ccar[...] = c[tm - 1:tm, :]
    _store_decay_slots(c * LOG2E, fq_ref, fk_ref, fqa_ref, fka_ref)

    xbuf[halo:halo + tm, :] = lx_ref[...]
    xc = cb_ref[...] + cw_ref[LRU_CONV - 1:LRU_CONV, :] * lx_ref[...]
    for k in range(LRU_CONV - 1):
        back = LRU_CONV - 1 - k
        xc = xc + cw_ref[k:k + 1, :] * xbuf[halo - back:halo - back + tm, :]
    xbuf[0:halo, :] = xbuf[tm:tm + halo, :]

    xcb = xc.astype(BF16)
    half = LRU_WIDTH // 2
    gates = [_dot(xcb[:, j * half:(j + 1) * half], wri_ref[j]) for j in range(2)]
    r = jax.nn.sigmoid(jnp.concatenate([g[:, :half] for g in gates], axis=1) + br_ref[...])
    i = jax.nn.sigmoid(jnp.concatenate([g[:, half:] for g in gates], axis=1) + bi_ref[...])
    log_a = (-LRU_C) * r * _softplus(-lam_ref[...])
    a = jnp.exp(log_a)
    b = jnp.sqrt(-jnp.tanh(log_a) * (a * a + 1.0)) * (i * xc)

    rowl = lax.broadcasted_iota(jnp.int32, (tm, LRU_WIDTH), 0)
    shift = 1
    while shift < tm:
        keep = rowl >= shift
        a_sh = jnp.where(keep, pltpu.roll(a, shift, axis=0), 1.0)
        b_sh = jnp.where(keep, pltpu.roll(b, shift, axis=0), 0.0)
        b = a * b_sh + b
        a = a * a_sh
        shift *= 2
    h = b + a * hcar[...]
    hcar[...] = h[tm - 1:tm, :]

    o = h * jax.nn.gelu(lg_ref[...], approximate=True)
    o_ref[...] = _rms(o, g_ref[...]).astype(BF16)


def _scan(fl, fq, fk, lx, lg, bf_row, cw, cb, wri, br, bi, lam, g_lru, batch, seq):
    tm = ATT_BLOCK
    ns = seq // tm
    rows = lambda c: pl.BlockSpec((tm, c), lambda b, s: (b * ns + s, 0))
    weights = (bf_row, cw, cb, wri, br, bi, lam, g_lru)
    t = batch * seq
    return pl.pallas_call(
        _scan_kernel,
        grid=(batch, ns),
        in_specs=[rows(LANES), rows(FOX_OUT), rows(FOX_OUT), rows(LRU_WIDTH), rows(LRU_WIDTH)]
        + [_const_spec(w.shape) for w in weights],
        out_specs=[rows(FOX_HEADS * LANES), rows(FOX_HEADS * LANES), rows(LRU_WIDTH)],
        out_shape=[jax.ShapeDtypeStruct((t, FOX_HEADS * LANES), BF16),
                   jax.ShapeDtypeStruct((t, FOX_HEADS * LANES), BF16),
                   jax.ShapeDtypeStruct((t, LRU_WIDTH), BF16)],
        scratch_shapes=[pltpu.VMEM((tm + SUBLANES, LRU_WIDTH), F32),
                        pltpu.VMEM((1, LRU_WIDTH), F32),
                        pltpu.VMEM((1, LANES), F32)],
        compiler_params=pltpu.CompilerParams(
            dimension_semantics=("parallel", "arbitrary"), vmem_limit_bytes=_vmem_limit(32 << 20)),
        name="scan",
    )(fl, fq, fk, lx, lg, *weights)


def _flash_heads(q_sc, load_k, load_v, qi, blk, s_sc, p_sc, m_sc, alpha_sc, acc_sc):
    nh = q_sc.shape[0]
    m_sc[...] = jnp.full(m_sc.shape, -jnp.inf, F32)
    acc_sc[...] = jnp.zeros(acc_sc.shape, F32)
    row = lax.broadcasted_iota(jnp.int32, (blk, blk), 0)
    col = lax.broadcasted_iota(jnp.int32, (blk, blk), 1)

    def scores(h, j, diagonal):
        s = lax.dot_general(q_sc[h], load_k(h, j), (((1,), (1,)), ((), ())), preferred_element_type=F32)
        if diagonal:
            s = jnp.where(col <= row, s, -jnp.inf)
        s_sc[h] = s

    def softmax(h):
        m_old = m_sc[h]
        m_blk = jnp.max(s_sc[h], axis=-1, keepdims=True)
        m_new = jnp.maximum(m_old, jnp.broadcast_to(m_blk, (blk, LANES)))
        m_sc[h] = m_new
        alpha_sc[h] = jnp.exp2(m_old - m_new)
        for c in range(0, blk, LANES):
            p_sc[h, :, c:c + LANES] = jnp.exp2(s_sc[h, :, c:c + LANES] - m_new).astype(BF16)

    def weighted_values(h, j):
        acc_sc[h] = alpha_sc[h] * acc_sc[h] + _dot(p_sc[h], load_v(h, j))

    def block(j, diagonal):
        for h in range(nh):
            scores(h, j, diagonal)
        for h in range(nh):
            softmax(h)
            weighted_values(h, j)

    def body(j, carry):
        block(j, False)
        return carry

    lax.fori_loop(0, qi, body, 0)
    block(qi, True)
    return [acc_sc[h] for h in range(nh)]


def _pair_norm_store(o_ref, accs, g_ref):
    half = LANES // 2
    lane = lax.broadcasted_iota(jnp.int32, accs[0].shape, 1)
    heads = [a / pltpu.roll(a, half, axis=1) for a in accs]
    pairs = [jnp.where(lane < half, heads[i], heads[i + 1]) for i in range(0, len(heads), 2)]
    n = len(pairs) * LANES
    ssq = sum(jnp.sum(p * p, axis=-1, keepdims=True) for p in pairs)
    rs = lax.rsqrt(ssq * (1.0 / n) + EPS)
    for i, p in enumerate(pairs):
        o_ref[:, i * LANES:(i + 1) * LANES] = (p * rs * g_ref[:, i * LANES:(i + 1) * LANES]).astype(BF16)


def _kv_rows(j, blk):
    return pl.ds(pl.multiple_of(j * blk, blk), blk)


def _mla_attn_kernel(q_ref, k_ref, v_ref, g_ref, o_ref, q_sc, *scratch):
    blk = q_ref.shape[0]
    lane = lax.broadcasted_iota(jnp.int32, (blk, 2 * LANES), 1)
    for h in range(MLA_HEADS):
        p, e = divmod(h, 2)
        x1 = LANES + MLA_HALF * h
        x2 = x1 + LANES // 2
        mine = (((lane >= MLA_NOPE * e) & (lane < MLA_NOPE * (e + 1)))
                | ((lane >= x1) & (lane < x1 + MLA_HALF))
                | ((lane >= x2) & (lane < x2 + MLA_HALF)))
        q_pair = q_ref[:, 2 * LANES * p:2 * LANES * (p + 1)]
        q_sc[h] = jnp.where(mine, q_pair, jnp.zeros_like(q_pair))
    load_k = lambda h, j: k_ref[_kv_rows(j, blk), 2 * LANES * (h // 2):2 * LANES * (h // 2 + 1)]
    load_v = lambda h, j: v_ref[_kv_rows(j, blk), LANES * h:LANES * (h + 1)]
    accs = _flash_heads(q_sc, load_k, load_v, pl.program_id(1), blk, *scratch)
    _pair_norm_store(o_ref, accs, g_ref)


def _fox_attn_kernel(q_ref, k_ref, v_ref, g_ref, o_ref, q_sc, *scratch):
    blk = q_ref.shape[0]
    for h in range(FOX_HEADS):
        q_sc[h] = q_ref[:, LANES * h:LANES * (h + 1)]
    load_k = lambda h, j: k_ref[_kv_rows(j, blk), LANES * h:LANES * (h + 1)]
    load_v = lambda h, j: v_ref[_kv_rows(j, blk), LANES * h:LANES * (h + 1)]
    accs = _flash_heads(q_sc, load_k, load_v, pl.program_id(1), blk, *scratch)
    _pair_norm_store(o_ref, accs, g_ref)


def _attention(kernel, name, q, k, v, gain, batch, seq, heads, dk):
    blk = ATT_BLOCK
    nq = seq // blk
    tile = lambda a: pl.BlockSpec((None, blk, a.shape[-1]), lambda b, i: (b, i, 0))
    full = lambda a: pl.BlockSpec((None,) + a.shape[1:], lambda b, i: (b,) + (0,) * (a.ndim - 1))
    q3, k3, v3 = (a.reshape(batch, seq, a.shape[-1]) for a in (q, k, v))
    d_out = v.shape[-1] // 2
    out = pl.pallas_call(
        kernel,
        grid=(batch, nq),
        in_specs=[tile(q3), full(k3), full(v3), _const_spec(gain.shape)],
        out_specs=pl.BlockSpec((None, blk, d_out), lambda b, i: (b, i, 0)),
        out_shape=jax.ShapeDtypeStruct((batch, seq, d_out), BF16),
        scratch_shapes=[pltpu.VMEM((heads, blk, dk), BF16),
                        pltpu.VMEM((heads, blk, blk), F32),
                        pltpu.VMEM((heads, blk, blk), BF16),
                        pltpu.VMEM((heads, blk, LANES), F32),
                        pltpu.VMEM((heads, blk, LANES), F32),
                        pltpu.VMEM((heads, blk, LANES), F32)],
        compiler_params=pltpu.CompilerParams(
            dimension_semantics=("parallel", "arbitrary"), vmem_limit_bytes=_vmem_limit(40 << 20)),
        name=name,
    )(q3, k3, v3, gain)
    return out.reshape(batch * seq, d_out)


def _ffn_kernel(om_ref, of_ref, ol_ref, h_ref, p_ref, wo_ref, gffn_ref, wup_ref, cw_ref, cb_ref, wd_ref,
                gple_ref, wgate_ref, wproj_ref, gfin_ref, out_ref,
                xn_sc, ubuf_g, ubuf_v, tail_g, tail_v, acc_ref, act_buf, *, final):
    tm = h_ref.shape[0]
    halo = SUBLANES

    @pl.when(pl.program_id(1) == 0)
    def _():
        tail_g[...] = jnp.zeros_like(tail_g)
        tail_v[...] = jnp.zeros_like(tail_v)

    a, b = om_ref.shape[1], om_ref.shape[1] + of_ref.shape[1]
    h1 = h_ref[...] + (_dot(om_ref[...], wo_ref[0:a, :]) + _dot(of_ref[...], wo_ref[a:b, :])
                       + _dot(ol_ref[...], wo_ref[b:, :]))
    acc_ref[...] = h1
    xn_sc[...] = _rms(h1, gffn_ref[...]).astype(BF16)

    def cols(c, part):
        lo = part * D_FF + c * FF_CHUNK
        return slice(lo, lo + FF_CHUNK)

    def up(c):
        xn = xn_sc[...]
        ubuf_g[c % 2, halo:halo + tm, :] = _dot(xn, wup_ref[:, cols(c, 0)])
        ubuf_v[c % 2, halo:halo + tm, :] = _dot(xn, wup_ref[:, cols(c, 1)])

    def conv(buf, r0, cw, cb):
        taps = [cw[k:k + 1, :] * buf[halo + r0 - (FFN_CONV - 1 - k):halo + r0 - (FFN_CONV - 1 - k) + FFN_ROW_BLOCK, :]
                for k in range(FFN_CONV)]
        return cb + taps[0] + taps[1] + taps[2]

    def gate(c):
        bg, bv = ubuf_g.at[c % 2], ubuf_v.at[c % 2]
        bg[0:halo, :] = tail_g[c]
        bv[0:halo, :] = tail_v[c]
        cwg, cwv = cw_ref[:, cols(c, 0)], cw_ref[:, cols(c, 1)]
        cbg, cbv = cb_ref[:, cols(c, 0)], cb_ref[:, cols(c, 1)]
        for r0 in range(0, tm, FFN_ROW_BLOCK):
            g = conv(bg, r0, cwg, cbg)
            v = conv(bv, r0, cwv, cbv)
            act_buf[c % 2, r0:r0 + FFN_ROW_BLOCK, :] = (g * jax.nn.sigmoid(g) * v).astype(BF16)
        tail_g[c] = bg[tm:tm + halo, :]
        tail_v[c] = bv[tm:tm + halo, :]

    def down(c):
        acc_ref[...] += _dot(act_buf[c % 2], wd_ref[c * FF_CHUNK:(c + 1) * FF_CHUNK, :])

    up(0)
    for c in range(N_FF_CHUNKS):
        if c + 1 < N_FF_CHUNKS:
            up(c + 1)
        if c >= 1:
            down(c - 1)
        gate(c)
    down(N_FF_CHUNKS - 1)

    h2 = acc_ref[...]
    ple_gate = jax.nn.sigmoid(_dot(_rms(h2, gple_ref[...]).astype(BF16), wgate_ref[...]))
    h3 = h2 + ple_gate * _dot(p_ref[...].astype(BF16), wproj_ref[...])
    out_ref[...] = _rms(h3, gfin_ref[...]) if final else h3


def _ffn(om, of, ol, h, p_l, wo, gffn, wup, cw, cb, wd, gple, wgate, wproj, gfin, batch, seq, final):
    tm = ROW_TILE
    ns = seq // tm
    t = batch * seq
    rows = lambda a: pl.BlockSpec((tm, a.shape[1]), lambda b, s: (b * ns + s, 0))
    tiles = (om, of, ol, h, p_l)
    weights = (wo, gffn, wup, cw, cb, wd, gple, wgate, wproj, gfin)
    return pl.pallas_call(
        functools.partial(_ffn_kernel, final=final),
        grid=(batch, ns),
        in_specs=[rows(a) for a in tiles] + [_const_spec(w.shape) for w in weights],
        out_specs=rows(h),
        out_shape=jax.ShapeDtypeStruct((t, D_MODEL), F32),
        scratch_shapes=[pltpu.VMEM((tm, D_MODEL), BF16),
                        pltpu.VMEM((2, tm + SUBLANES, FF_CHUNK), F32),
                        pltpu.VMEM((2, tm + SUBLANES, FF_CHUNK), F32),
                        pltpu.VMEM((N_FF_CHUNKS, SUBLANES, FF_CHUNK), F32),
                        pltpu.VMEM((N_FF_CHUNKS, SUBLANES, FF_CHUNK), F32),
                        pltpu.VMEM((tm, D_MODEL), F32),
                        pltpu.VMEM((2, tm, FF_CHUNK), BF16)],
        compiler_params=pltpu.CompilerParams(
            dimension_semantics=("parallel", "arbitrary"), vmem_limit_bytes=_vmem_limit(48 << 20)),
        name="ffn_ple",
    )(*tiles, *weights)


def _pack_w_in(w):
    o_kvc = MLA_Q_RANK
    o_kr = o_kvc + MLA_KV_RANK
    o_fox = o_kr + MLA_ROPE
    o_fl = o_fox + 3 * FOX_OUT
    o_lru = o_fl + FOX_HEADS
    pad = jnp.zeros((w.shape[0], ZC_KV - MLA_Q_RANK - FOX_HEADS), w.dtype)
    kr1 = jnp.tile(w[:, o_kr:o_kr + MLA_HALF], (1, MLA_HEADS))
    kr2 = jnp.tile(w[:, o_kr + MLA_HALF:o_fox], (1, MLA_HEADS))
    return jnp.concatenate(
        [w[:, :MLA_Q_RANK], w[:, o_fl:o_lru], pad, w[:, o_kvc:o_kr], kr1, kr2, w[:, o_fox:o_fl], w[:, o_lru:]],
        axis=1).astype(BF16)


def _pack_w_uq(w):
    w3 = w.reshape(MLA_Q_RANK, MLA_HEADS, MLA_NOPE + MLA_ROPE)
    nope = w3[:, :, :MLA_NOPE].reshape(MLA_Q_RANK, -1)
    x1 = w3[:, :, MLA_NOPE:MLA_NOPE + MLA_HALF].reshape(MLA_Q_RANK, -1)
    x2 = w3[:, :, MLA_NOPE + MLA_HALF:].reshape(MLA_Q_RANK, -1)
    packed = jnp.concatenate([nope, x1, x2], axis=1)
    return jnp.pad(packed, ((0, 2 * LANES - MLA_Q_RANK), (0, 0))).astype(BF16)


def _pack_w_ukv(w):
    w3 = w.reshape(MLA_KV_RANK, MLA_HEADS, MLA_NOPE + MLA_V)
    return jnp.concatenate([w3[:, :, :MLA_NOPE].reshape(MLA_KV_RANK, -1),
                            w3[:, :, MLA_NOPE:].reshape(MLA_KV_RANK, -1)], axis=1).astype(BF16)


def _pack_gates(w_r, w_i):
    def block_diag(w):
        n = w.shape[0]
        eye = jnp.eye(n, dtype=w.dtype)
        return jnp.einsum("ncd,nm->ncmd", w, eye).reshape(n * LRU_BLOCK, n * LRU_BLOCK)
    hb = LRU_BLOCKS // 2
    return jnp.stack([jnp.concatenate([block_diag(w_r[j * hb:(j + 1) * hb]),
                                       block_diag(w_i[j * hb:(j + 1) * hb])], axis=1)
                      for j in range(2)]).astype(BF16)


def kernel(x, p, positions, g_mix, w_in, g_qc, w_uq, g_kvc, w_ukv, b_f, lru_conv_w, lru_conv_b, w_r, b_r, w_i, b_i, lru_lambda, g_out, w_o, g_ffn, w_up, ffn_conv_w, ffn_conv_b, w_down, g_ple, w_ple_gate, w_ple_proj, g_final):
    batch, seq, _ = x.shape
    depth = w_in.shape[0]
    t = batch * seq
    row = lambda v: v.reshape(1, -1)

    cos, sin = _rope_tables(positions)
    h = x.reshape(t, D_MODEL)
    for l in range(depth):
        gqc = jnp.pad(g_qc[l], (0, 2 * LANES - MLA_Q_RANK))
        qm, km, vm, fq, fk, fv, fl, lx, lg = _proj(
            h, row(g_mix[l]), _pack_w_in(w_in[l]), row(gqc), _pack_w_uq(w_uq[l]), row(g_kvc[l]),
            _pack_w_ukv(w_ukv[l]), cos, sin)

        bf_row = jnp.pad(b_f[l], (FLOGIT_LANE, LANES - FLOGIT_LANE - FOX_HEADS))
        g_mla, g_fox, g_lru = (g_out[l, :2 * LANES], g_out[l, 2 * LANES:4 * LANES], g_out[l, 4 * LANES:])
        fqa, fka, o_lru = _scan(fl, fq, fk, lx, lg, row(bf_row), lru_conv_w[l], row(lru_conv_b[l]),
                              _pack_gates(w_r[l], w_i[l]), row(b_r[l]), row(b_i[l]), row(lru_lambda[l]),
                              row(g_lru), batch, seq)

        o_mla = _attention(_mla_attn_kernel, "mla_attn", qm, km, vm, row(g_mla), batch, seq,
                           MLA_HEADS, 2 * LANES)
        o_fox = _attention(_fox_attn_kernel, "fox_attn", fqa, fka, fv, row(g_fox), batch, seq,
                           FOX_HEADS, LANES)

        h = _ffn(o_mla, o_fox, o_lru, h, p[l].reshape(t, PLE_DIM), w_o[l].astype(BF16), row(g_ffn[l]),
                 w_up[l].astype(BF16), ffn_conv_w[l], row(ffn_conv_b[l]), w_down[l].astype(BF16),
                 row(g_ple[l]), w_ple_gate[l].astype(BF16), w_ple_proj[l].astype(BF16), row(g_final),
                 batch, seq, final=(l == depth - 1))
    return h.reshape(batch, seq, D_MODEL)
```

```python
import functools
import math

import jax
import jax.numpy as jnp
from jax import lax
from jax.experimental import pallas as pl
from jax.experimental.pallas import tpu as pltpu

F32 = jnp.float32
BF16 = jnp.bfloat16

D_MODEL = 1024
PLE_DIM = 256
MLA_HEADS = 4
MLA_NOPE = 64
MLA_ROPE = 32
MLA_HALF = MLA_ROPE // 2
MLA_V = 64
MLA_Q_RANK = 192
MLA_KV_RANK = 128
FOX_HEADS = 4
FOX_HEAD_DIM = 64
FOX_OUT = FOX_HEADS * FOX_HEAD_DIM
LRU_WIDTH = 512
LRU_BLOCKS = 8
LRU_BLOCK = LRU_WIDTH // LRU_BLOCKS
LRU_CONV = 4
LRU_C = 8.0
D_FF = 2816
FFN_CONV = 3
ROPE_THETA = 10000.0
EPS = 1e-6
LOG2E = math.log2(math.e)

LANES = 128
SUBLANES = 8
V7X_VMEM_BYTES = 64 << 20

ROW_TILE = 512
ATT_BLOCK = 512
FF_CHUNK = 256
N_FF_CHUNKS = D_FF // FF_CHUNK
FFN_ROW_BLOCK = 64

ZC_G0 = 0
ZC_KV = 256
ZC_FOX = 512
ZC_LRU = 1280
ZC_END = 2304
FLOGIT_LANE = MLA_Q_RANK - LANES

MLA_QSCALE = (MLA_NOPE + MLA_ROPE) ** -0.5 * LOG2E
FOX_QSCALE = FOX_HEAD_DIM ** -0.5 * LOG2E


def _vmem_limit(nbytes):
    return int(min(nbytes + (8 << 20), V7X_VMEM_BYTES - (4 << 20)))


def _const_spec(shape):
    nd = len(shape)
    return pl.BlockSpec(shape, lambda *_: (0,) * nd, pipeline_mode=pl.Buffered(1))


def _layer_spec(stacked, layer):
    nd = stacked.ndim - 1
    return pl.BlockSpec((None,) + stacked.shape[1:], lambda *_: (layer,) + (0,) * nd,
                        pipeline_mode=pl.Buffered(1))


def _rms(x, gain, n=None):
    n = x.shape[-1] if n is None else n
    return x * lax.rsqrt(jnp.sum(x * x, axis=-1, keepdims=True) * (1.0 / n) + EPS) * gain


def _dot(a, b):
    return jnp.dot(a, b, preferred_element_type=F32)


def _softplus(x):
    return jnp.maximum(x, 0.0) + jnp.log1p(jnp.exp(-jnp.abs(x)))


def _rope_table_kernel(pos_ref, freq_ref, sign_ref, cos_ref, sin_ref):
    ang = pos_ref[...].astype(F32) * freq_ref[...]
    cos_ref[...] = jnp.cos(ang)
    sin_ref[...] = jnp.sin(ang) * sign_ref[...]


def _rope_tables(positions):
    t = positions.size
    half = MLA_HALF
    freqs = ROPE_THETA ** (-jnp.arange(half, dtype=F32) / half)
    freq_row = jnp.tile(freqs, LANES // half)[None, :]
    sign_row = jnp.where(jnp.arange(LANES) < LANES // 2, -1.0, 1.0).astype(F32)[None, :]
    tm = ROW_TILE
    return pl.pallas_call(
        _rope_table_kernel,
        grid=(t // tm,),
        in_specs=[pl.BlockSpec((tm, 1), lambda i: (i, 0)),
                  _const_spec((1, LANES)), _const_spec((1, LANES))],
        out_specs=[pl.BlockSpec((tm, LANES), lambda i: (i, 0))] * 2,
        out_shape=[jax.ShapeDtypeStruct((t, LANES), F32)] * 2,
        name="rope_tables",
    )(positions.reshape(t, 1), freq_row, sign_row)


def _store_value_slots(ref, v_pairs):
    half = LANES // 2
    lane = lax.broadcasted_iota(jnp.int32, (v_pairs.shape[0], LANES), 1)
    for h in range(v_pairs.shape[1] // half):
        p, e = divmod(h, 2)
        own = (lane >= half * e) & (lane < half * (e + 1))
        ref[:, LANES * h:LANES * (h + 1)] = jnp.where(own, v_pairs[:, LANES * p:LANES * (p + 1)], 1.0).astype(BF16)


def _proj_kernel(h_ref, gmix_ref, win_ref, gqc_ref, wq_ref, gkvc_ref, wkv_ref, cos_ref, sin_ref,
                 qm_ref, km_ref, vm_ref, fq_ref, fk_ref, fv_ref, fl_ref, lx_ref, lg_ref):
    xn = _rms(h_ref[...], gmix_ref[...]).astype(BF16)
    cos = cos_ref[...]
    sin = sin_ref[...]

    def zcols(lo, hi):
        return _dot(xn, win_ref[:, lo:hi])

    def rotate(r):
        return r * cos + pltpu.roll(r, LANES // 2, axis=1) * sin

    g0 = zcols(ZC_G0, ZC_KV)
    kvz = zcols(ZC_KV, ZC_FOX)
    f = zcols(ZC_FOX, ZC_LRU)
    fq_ref[...] = (f[:, :FOX_OUT] * FOX_QSCALE).astype(BF16)
    fk_ref[...] = f[:, FOX_OUT:2 * FOX_OUT].astype(BF16)
    _store_value_slots(fv_ref, f[:, 2 * FOX_OUT:])
    lz = zcols(ZC_LRU, ZC_END)
    lx_ref[...] = lz[:, :LRU_WIDTH]
    lg_ref[...] = lz[:, LRU_WIDTH:]

    lane = lax.broadcasted_iota(jnp.int32, g0.shape, 1)
    qc = jnp.where(lane < MLA_Q_RANK, g0, 0.0)
    qcn = _rms(qc, gqc_ref[...], MLA_Q_RANK).astype(BF16)
    qq = _dot(qcn, wq_ref[...]) * MLA_QSCALE
    q_rot = rotate(qq[:, 2 * LANES:]).astype(BF16)
    qm_ref[:, 0:LANES] = qq[:, 0:LANES].astype(BF16)
    qm_ref[:, LANES:2 * LANES] = q_rot
    qm_ref[:, 2 * LANES:3 * LANES] = qq[:, LANES:2 * LANES].astype(BF16)
    qm_ref[:, 3 * LANES:] = q_rot
    fl_ref[...] = g0[:, LANES:]

    kvn = _rms(kvz[:, :MLA_KV_RANK], gkvc_ref[...]).astype(BF16)
    kv = _dot(kvn, wkv_ref[...])
    k_rot = rotate(kvz[:, MLA_KV_RANK:]).astype(BF16)
    km_ref[:, 0:LANES] = kv[:, 0:LANES].astype(BF16)
    km_ref[:, LANES:2 * LANES] = k_rot
    km_ref[:, 2 * LANES:3 * LANES] = kv[:, LANES:2 * LANES].astype(BF16)
    km_ref[:, 3 * LANES:] = k_rot
    _store_value_slots(vm_ref, kv[:, 2 * LANES:])


def _proj(h, gmix, win, gqc, wq, gkvc, wkv, cos, sin, layer):
    t = h.shape[0]
    tm = ROW_TILE
    row = lambda c: pl.BlockSpec((tm, c), lambda i: (i, 0))
    outs = [(4 * LANES, BF16), (4 * LANES, BF16), (MLA_HEADS * LANES, BF16),
            (FOX_OUT, BF16), (FOX_OUT, BF16), (FOX_HEADS * LANES, BF16),
            (LANES, F32), (LRU_WIDTH, F32), (LRU_WIDTH, F32)]
    weights = (gmix, win, gqc, wq, gkvc, wkv)
    return pl.pallas_call(
        _proj_kernel,
        grid=(t // tm,),
        in_specs=[row(D_MODEL)] + [_layer_spec(w, layer) for w in weights] + [row(LANES), row(LANES)],
        out_specs=[row(c) for c, _ in outs],
        out_shape=[jax.ShapeDtypeStruct((t, c), d) for c, d in outs],
        compiler_params=pltpu.CompilerParams(
            dimension_semantics=("parallel",), vmem_limit_bytes=_vmem_limit(40 << 20)),
        name="proj",
    )(h, *weights, cos, sin)


def _store_decay_slots(c2, fq_ref, fk_ref, fqa_ref, fka_ref):
    tm = c2.shape[0]
    half = LANES // 2
    lane = lax.broadcasted_iota(jnp.int32, (tm, LANES), 1)
    for h in range(FOX_HEADS):
        p, e = divmod(h, 2)
        ch = jnp.broadcast_to(c2[:, FLOGIT_LANE + h:FLOGIT_LANE + h + 1], (tm, LANES))
        hi = ch.astype(BF16).astype(F32)
        rest = ch - hi
        mid = rest.astype(BF16).astype(F32)
        lo = rest - mid
        parts = (hi, mid, lo)
        base = half * (1 - e)
        aug_q = jnp.where((lane >= base + 3) & (lane < base + 6), 1.0, 0.0)
        aug_k = jnp.where((lane >= base) & (lane < base + 3), 1.0, 0.0)
        for i, part in enumerate(parts):
            aug_q = jnp.where(lane == base + i, part, aug_q)
            aug_k = jnp.where(lane == base + 3 + i, -part, aug_k)
        own = (lane >= half * e) & (lane < half * (e + 1))
        cols = slice(LANES * p, LANES * (p + 1))
        slot = slice(LANES * h, LANES * (h + 1))
        fqa_ref[:, slot] = jnp.where(own, fq_ref[:, cols], aug_q.astype(BF16))
        fka_ref[:, slot] = jnp.where(own, fk_ref[:, cols], aug_k.astype(BF16))


def _scan_kernel(fl_ref, fq_ref, fk_ref, lx_ref, lg_ref, bf_ref, cw_ref, cb_ref, wri_ref, br_ref, bi_ref,
                 lam_ref, g_ref, fqa_ref, fka_ref, o_ref, xbuf, hcar, ccar):
    tm = lx_ref.shape[0]
    halo = SUBLANES

    @pl.when(pl.program_id(1) == 0)
    def _():
        xbuf[0:halo, :] = jnp.zeros((halo, LRU_WIDTH), F32)
        hcar[...] = jnp.zeros_like(hcar)
        ccar[...] = jnp.zeros_like(ccar)

    lane = lax.broadcasted_iota(jnp.int32, (tm, LANES), 1)
    row = lax.broadcasted_iota(jnp.int32, (tm, LANES), 0)
    valid = (lane >= FLOGIT_LANE) & (lane < FLOGIT_LANE + FOX_HEADS)
    c = jnp.where(valid, -_softplus(-(fl_ref[...] + bf_ref[...])), 0.0)
    shift = 1
    while shift < tm:
        c = c + jnp.where(row >= shift, pltpu.roll(c, shift, axis=0), 0.0)
        shift *= 2
    c = c + ccar[...]
    ccar[...] = c[tm - 1:tm, :]
    _store_decay_slots(c * LOG2E, fq_ref, fk_ref, fqa_ref, fka_ref)

    xbuf[halo:halo + tm, :] = lx_ref[...]
    xc = cb_ref[...] + cw_ref[LRU_CONV - 1:LRU_CONV, :] * lx_ref[...]
    for k in range(LRU_CONV - 1):
        back = LRU_CONV - 1 - k
        xc = xc + cw_ref[k:k + 1, :] * xbuf[halo - back:halo - back + tm, :]
    xbuf[0:halo, :] = xbuf[tm:tm + halo, :]

    xcb = xc.astype(BF16)
    half = LRU_WIDTH // 2
    gates = [_dot(xcb[:, j * half:(j + 1) * half], wri_ref[j]) for j in range(2)]
    r = jax.nn.sigmoid(jnp.concatenate([g[:, :half] for g in gates], axis=1) + br_ref[...])
    i = jax.nn.sigmoid(jnp.concatenate([g[:, half:] for g in gates], axis=1) + bi_ref[...])
    log_a = (-LRU_C) * r * _softplus(-lam_ref[...])
    a = jnp.exp(log_a)
    b = jnp.sqrt(-jnp.tanh(log_a) * (a * a + 1.0)) * (i * xc)

    rowl = lax.broadcasted_iota(jnp.int32, (tm, LRU_WIDTH), 0)
    shift = 1
    while shift < tm:
        keep = rowl >= shift
        a_sh = jnp.where(keep, pltpu.roll(a, shift, axis=0), 1.0)
        b_sh = jnp.where(keep, pltpu.roll(b, shift, axis=0), 0.0)
        b = a * b_sh + b
        a = a * a_sh
        shift *= 2
    h = b + a * hcar[...]
    hcar[...] = h[tm - 1:tm, :]

    o = h * jax.nn.gelu(lg_ref[...], approximate=True)
    o_ref[...] = _rms(o, g_ref[...]).astype(BF16)


def _scan(fl, fq, fk, lx, lg, bf_row, cw, cb, wri, br, bi, lam, g_lru, batch, seq, layer):
    tm = ATT_BLOCK
    ns = seq // tm
    rows = lambda c: pl.BlockSpec((tm, c), lambda b, s: (b * ns + s, 0))
    weights = (bf_row, cw, cb, wri, br, bi, lam, g_lru)
    t = batch * seq
    return pl.pallas_call(
        _scan_kernel,
        grid=(batch, ns),
        in_specs=[rows(LANES), rows(FOX_OUT), rows(FOX_OUT), rows(LRU_WIDTH), rows(LRU_WIDTH)]
        + [_layer_spec(w, layer) for w in weights],
        out_specs=[rows(FOX_HEADS * LANES), rows(FOX_HEADS * LANES), rows(LRU_WIDTH)],
        out_shape=[jax.ShapeDtypeStruct((t, FOX_HEADS * LANES), BF16),
                   jax.ShapeDtypeStruct((t, FOX_HEADS * LANES), BF16),
                   jax.ShapeDtypeStruct((t, LRU_WIDTH), BF16)],
        scratch_shapes=[pltpu.VMEM((tm + SUBLANES, LRU_WIDTH), F32),
                        pltpu.VMEM((1, LRU_WIDTH), F32),
                        pltpu.VMEM((1, LANES), F32)],
        compiler_params=pltpu.CompilerParams(
            dimension_semantics=("parallel", "arbitrary"), vmem_limit_bytes=_vmem_limit(32 << 20)),
        name="scan",
    )(fl, fq, fk, lx, lg, *weights)


def _flash_heads(q_sc, load_k, load_v, qi, blk, s_sc, p_sc, m_sc, alpha_sc, acc_sc):
    nh = q_sc.shape[0]
    m_sc[...] = jnp.full(m_sc.shape, -jnp.inf, F32)
    acc_sc[...] = jnp.zeros(acc_sc.shape, F32)
    row = lax.broadcasted_iota(jnp.int32, (blk, blk), 0)
    col = lax.broadcasted_iota(jnp.int32, (blk, blk), 1)

    def scores(h, j, diagonal):
        s = lax.dot_general(q_sc[h], load_k(h, j), (((1,), (1,)), ((), ())), preferred_element_type=F32)
        if diagonal:
            s = jnp.where(col <= row, s, -jnp.inf)
        s_sc[h] = s

    def softmax(h):
        m_old = m_sc[h]
        m_blk = jnp.max(s_sc[h], axis=-1, keepdims=True)
        m_new = jnp.maximum(m_old, jnp.broadcast_to(m_blk, (blk, LANES)))
        m_sc[h] = m_new
        alpha_sc[h] = jnp.exp2(m_old - m_new)
        for c in range(0, blk, LANES):
            p_sc[h, :, c:c + LANES] = jnp.exp2(s_sc[h, :, c:c + LANES] - m_new).astype(BF16)

    def weighted_values(h, j):
        acc_sc[h] = alpha_sc[h] * acc_sc[h] + _dot(p_sc[h], load_v(h, j))

    def block(j, diagonal):
        for h in range(nh):
            scores(h, j, diagonal)
        for h in range(nh):
            softmax(h)
            weighted_values(h, j)

    def body(j, carry):
        block(j, False)
        return carry

    lax.fori_loop(0, qi, body, 0)
    block(qi, True)
    return [acc_sc[h] for h in range(nh)]


def _pair_norm_store(o_ref, accs, g_ref):
    half = LANES // 2
    lane = lax.broadcasted_iota(jnp.int32, accs[0].shape, 1)
    heads = [a / pltpu.roll(a, half, axis=1) for a in accs]
    pairs = [jnp.where(lane < half, heads[i], heads[i + 1]) for i in range(0, len(heads), 2)]
    n = len(pairs) * LANES
    ssq = sum(jnp.sum(p * p, axis=-1, keepdims=True) for p in pairs)
    rs = lax.rsqrt(ssq * (1.0 / n) + EPS)
    for i, p in enumerate(pairs):
        o_ref[:, i * LANES:(i + 1) * LANES] = (p * rs * g_ref[:, i * LANES:(i + 1) * LANES]).astype(BF16)


def _kv_rows(j, blk):
    return pl.ds(pl.multiple_of(j * blk, blk), blk)


def _mla_attn_kernel(q_ref, k_ref, v_ref, g_ref, o_ref, q_sc, *scratch):
    blk = q_ref.shape[0]
    lane = lax.broadcasted_iota(jnp.int32, (blk, 2 * LANES), 1)
    for h in range(MLA_HEADS):
        p, e = divmod(h, 2)
        x1 = LANES + MLA_HALF * h
        x2 = x1 + LANES // 2
        mine = (((lane >= MLA_NOPE * e) & (lane < MLA_NOPE * (e + 1)))
                | ((lane >= x1) & (lane < x1 + MLA_HALF))
                | ((lane >= x2) & (lane < x2 + MLA_HALF)))
        q_pair = q_ref[:, 2 * LANES * p:2 * LANES * (p + 1)]
        q_sc[h] = jnp.where(mine, q_pair, jnp.zeros_like(q_pair))
    load_k = lambda h, j: k_ref[_kv_rows(j, blk), 2 * LANES * (h // 2):2 * LANES * (h // 2 + 1)]
    load_v = lambda h, j: v_ref[_kv_rows(j, blk), LANES * h:LANES * (h + 1)]
    accs = _flash_heads(q_sc, load_k, load_v, pl.program_id(1), blk, *scratch)
    _pair_norm_store(o_ref, accs, g_ref)


def _fox_attn_kernel(q_ref, k_ref, v_ref, g_ref, o_ref, q_sc, *scratch):
    blk = q_ref.shape[0]
    for h in range(FOX_HEADS):
        q_sc[h] = q_ref[:, LANES * h:LANES * (h + 1)]
    load_k = lambda h, j: k_ref[_kv_rows(j, blk), LANES * h:LANES * (h + 1)]
    load_v = lambda h, j: v_ref[_kv_rows(j, blk), LANES * h:LANES * (h + 1)]
    accs = _flash_heads(q_sc, load_k, load_v, pl.program_id(1), blk, *scratch)
    _pair_norm_store(o_ref, accs, g_ref)


def _attention(kernel, name, q, k, v, gain, layer, batch, seq, heads, dk):
    blk = ATT_BLOCK
    nq = seq // blk
    tile = lambda a: pl.BlockSpec((None, blk, a.shape[-1]), lambda b, i: (b, i, 0))
    full = lambda a: pl.BlockSpec((None,) + a.shape[1:], lambda b, i: (b,) + (0,) * (a.ndim - 1))
    q3, k3, v3 = (a.reshape(batch, seq, a.shape[-1]) for a in (q, k, v))
    d_out = v.shape[-1] // 2
    out = pl.pallas_call(
        kernel,
        grid=(batch, nq),
        in_specs=[tile(q3), full(k3), full(v3), _layer_spec(gain, layer)],
        out_specs=pl.BlockSpec((None, blk, d_out), lambda b, i: (b, i, 0)),
        out_shape=jax.ShapeDtypeStruct((batch, seq, d_out), BF16),
        scratch_shapes=[pltpu.VMEM((heads, blk, dk), BF16),
                        pltpu.VMEM((heads, blk, blk), F32),
                        pltpu.VMEM((heads, blk, blk), BF16),
                        pltpu.VMEM((heads, blk, LANES), F32),
                        pltpu.VMEM((heads, blk, LANES), F32),
                        pltpu.VMEM((heads, blk, LANES), F32)],
        compiler_params=pltpu.CompilerParams(
            dimension_semantics=("parallel", "arbitrary"), vmem_limit_bytes=_vmem_limit(40 << 20)),
        name=name,
    )(q3, k3, v3, gain)
    return out.reshape(batch * seq, d_out)


def _ffn_kernel(om_ref, of_ref, ol_ref, h_ref, p_ref, wo_ref, gffn_ref, wup_ref, cw_ref, cb_ref, wd_ref,
                gple_ref, wgate_ref, wproj_ref, gfin_ref, out_ref,
                xn_sc, ubuf_g, ubuf_v, tail_g, tail_v, acc_ref, act_buf, *, final):
    tm = h_ref.shape[0]
    halo = SUBLANES

    @pl.when(pl.program_id(1) == 0)
    def _():
        tail_g[...] = jnp.zeros_like(tail_g)
        tail_v[...] = jnp.zeros_like(tail_v)

    a, b = om_ref.shape[1], om_ref.shape[1] + of_ref.shape[1]
    h1 = h_ref[...] + (_dot(om_ref[...], wo_ref[0:a, :]) + _dot(of_ref[...], wo_ref[a:b, :])
                       + _dot(ol_ref[...], wo_ref[b:, :]))
    acc_ref[...] = h1
    xn_sc[...] = _rms(h1, gffn_ref[...]).astype(BF16)

    def cols(c, part):
        lo = part * D_FF + c * FF_CHUNK
        return slice(lo, lo + FF_CHUNK)

    def up(c):
        xn = xn_sc[...]
        ubuf_g[c % 2, halo:halo + tm, :] = _dot(xn, wup_ref[:, cols(c, 0)])
        ubuf_v[c % 2, halo:halo + tm, :] = _dot(xn, wup_ref[:, cols(c, 1)])

    def conv(buf, r0, cw, cb):
        taps = [cw[k:k + 1, :] * buf[halo + r0 - (FFN_CONV - 1 - k):halo + r0 - (FFN_CONV - 1 - k) + FFN_ROW_BLOCK, :]
                for k in range(FFN_CONV)]
        return cb + taps[0] + taps[1] + taps[2]

    def gate(c):
        bg, bv = ubuf_g.at[c % 2], ubuf_v.at[c % 2]
        bg[0:halo, :] = tail_g[c]
        bv[0:halo, :] = tail_v[c]
        cwg, cwv = cw_ref[:, cols(c, 0)], cw_ref[:, cols(c, 1)]
        cbg, cbv = cb_ref[:, cols(c, 0)], cb_ref[:, cols(c, 1)]
        for r0 in range(0, tm, FFN_ROW_BLOCK):
            g = conv(bg, r0, cwg, cbg)
            v = conv(bv, r0, cwv, cbv)
            act_buf[r0:r0 + FFN_ROW_BLOCK, c * FF_CHUNK:(c + 1) * FF_CHUNK] = (g * jax.nn.sigmoid(g) * v).astype(BF16)
        tail_g[c] = bg[tm:tm + halo, :]
        tail_v[c] = bv[tm:tm + halo, :]

    up(0)
    for c in range(N_FF_CHUNKS):
        if c + 1 < N_FF_CHUNKS:
            up(c + 1)
        gate(c)

    h2 = acc_ref[...] + _dot(act_buf[...], wd_ref[...])
    ple_gate = jax.nn.sigmoid(_dot(_rms(h2, gple_ref[...]).astype(BF16), wgate_ref[...]))
    h3 = h2 + ple_gate * _dot(p_ref[...].astype(BF16), wproj_ref[...])
    out_ref[...] = _rms(h3, gfin_ref[...]) if final else h3


def _ffn(om, of, ol, h, p_all, wo, gffn, wup, cw, cb, wd, gple, wgate, wproj, gfin, batch, seq, layer, final):
    tm = ROW_TILE
    ns = seq // tm
    t = batch * seq
    rows = lambda a: pl.BlockSpec((tm, a.shape[1]), lambda b, s: (b * ns + s, 0))
    p_rows = pl.BlockSpec((tm, p_all.shape[1]), lambda b, s: (layer * (t // tm) + b * ns + s, 0))
    tiles = (om, of, ol, h)
    weights = (wo, gffn, wup, cw, cb, wd, gple, wgate, wproj)
    return pl.pallas_call(
        functools.partial(_ffn_kernel, final=final),
        grid=(batch, ns),
        in_specs=([rows(a) for a in tiles] + [p_rows] + [_layer_spec(w, layer) for w in weights]
                  + [_const_spec(gfin.shape)]),
        out_specs=rows(h),
        out_shape=jax.ShapeDtypeStruct((t, D_MODEL), F32),
        scratch_shapes=[pltpu.VMEM((tm, D_MODEL), BF16),
                        pltpu.VMEM((2, tm + SUBLANES, FF_CHUNK), F32),
                        pltpu.VMEM((2, tm + SUBLANES, FF_CHUNK), F32),
                        pltpu.VMEM((N_FF_CHUNKS, SUBLANES, FF_CHUNK), F32),
                        pltpu.VMEM((N_FF_CHUNKS, SUBLANES, FF_CHUNK), F32),
                        pltpu.VMEM((tm, D_MODEL), F32),
                        pltpu.VMEM((tm, D_FF), BF16)],
        compiler_params=pltpu.CompilerParams(
            dimension_semantics=("parallel", "arbitrary"), vmem_limit_bytes=_vmem_limit(48 << 20)),
        name="ffn_ple",
    )(*tiles, p_all, *weights, gfin)


def _pack_w_in(w):
    o_kvc = MLA_Q_RANK
    o_kr = o_kvc + MLA_KV_RANK
    o_fox = o_kr + MLA_ROPE
    o_fl = o_fox + 3 * FOX_OUT
    o_lru = o_fl + FOX_HEADS
    pad = jnp.zeros((w.shape[0], ZC_KV - MLA_Q_RANK - FOX_HEADS), w.dtype)
    kr1 = jnp.tile(w[:, o_kr:o_kr + MLA_HALF], (1, MLA_HEADS))
    kr2 = jnp.tile(w[:, o_kr + MLA_HALF:o_fox], (1, MLA_HEADS))
    return jnp.concatenate(
        [w[:, :MLA_Q_RANK], w[:, o_fl:o_lru], pad, w[:, o_kvc:o_kr], kr1, kr2, w[:, o_fox:o_fl], w[:, o_lru:]],
        axis=1).astype(BF16)


def _pack_w_uq(w):
    w3 = w.reshape(MLA_Q_RANK, MLA_HEADS, MLA_NOPE + MLA_ROPE)
    nope = w3[:, :, :MLA_NOPE].reshape(MLA_Q_RANK, -1)
    x1 = w3[:, :, MLA_NOPE:MLA_NOPE + MLA_HALF].reshape(MLA_Q_RANK, -1)
    x2 = w3[:, :, MLA_NOPE + MLA_HALF:].reshape(MLA_Q_RANK, -1)
    packed = jnp.concatenate([nope, x1, x2], axis=1)
    return jnp.pad(packed, ((0, 2 * LANES - MLA_Q_RANK), (0, 0))).astype(BF16)


def _pack_w_ukv(w):
    w3 = w.reshape(MLA_KV_RANK, MLA_HEADS, MLA_NOPE + MLA_V)
    return jnp.concatenate([w3[:, :, :MLA_NOPE].reshape(MLA_KV_RANK, -1),
                            w3[:, :, MLA_NOPE:].reshape(MLA_KV_RANK, -1)], axis=1).astype(BF16)


def _pack_gates(w_r, w_i):
    def block_diag(w):
        n = w.shape[0]
        eye = jnp.eye(n, dtype=w.dtype)
        return jnp.einsum("ncd,nm->ncmd", w, eye).reshape(n * LRU_BLOCK, n * LRU_BLOCK)
    hb = LRU_BLOCKS // 2
    return jnp.stack([jnp.concatenate([block_diag(w_r[j * hb:(j + 1) * hb]),
                                       block_diag(w_i[j * hb:(j + 1) * hb])], axis=1)
                      for j in range(2)]).astype(BF16)


def kernel(x, p, positions, g_mix, w_in, g_qc, w_uq, g_kvc, w_ukv, b_f, lru_conv_w, lru_conv_b, w_r, b_r, w_i, b_i, lru_lambda, g_out, w_o, g_ffn, w_up, ffn_conv_w, ffn_conv_b, w_down, g_ple, w_ple_gate, w_ple_proj, g_final):
    batch, seq, _ = x.shape
    depth = w_in.shape[0]
    t = batch * seq
    rows3 = lambda v: v.reshape(depth, 1, -1)
    bf16 = lambda w: w.astype(BF16)
    win, wq, wkv = jax.vmap(_pack_w_in)(w_in), jax.vmap(_pack_w_uq)(w_uq), jax.vmap(_pack_w_ukv)(w_ukv)
    gqc = rows3(jnp.pad(g_qc, ((0, 0), (0, 2 * LANES - MLA_Q_RANK))))
    bf_row = rows3(jnp.pad(b_f, ((0, 0), (FLOGIT_LANE, LANES - FLOGIT_LANE - FOX_HEADS))))
    wri = jax.vmap(_pack_gates)(w_r, w_i)
    g_mla, g_fox, g_lru = (rows3(g_out[:, :2 * LANES]), rows3(g_out[:, 2 * LANES:4 * LANES]),
                           rows3(g_out[:, 4 * LANES:]))
    wo, wup, wd, wgate, wproj = bf16(w_o), bf16(w_up), bf16(w_down), bf16(w_ple_gate), bf16(w_ple_proj)
    p_all = p.reshape(depth * t, PLE_DIM)

    cos, sin = _rope_tables(positions)
    h = x.reshape(t, D_MODEL)
    for l in range(depth):
        qm, km, vm, fq, fk, fv, fl, lx, lg = _proj(
            h, rows3(g_mix), win, gqc, wq, rows3(g_kvc), wkv, cos, sin, l)
        fqa, fka, o_lru = _scan(fl, fq, fk, lx, lg, bf_row, lru_conv_w, rows3(lru_conv_b), wri,
                                rows3(b_r), rows3(b_i), rows3(lru_lambda), g_lru, batch, seq, l)
        o_mla = _attention(_mla_attn_kernel, "mla_attn", qm, km, vm, g_mla, l, batch, seq,
                           MLA_HEADS, 2 * LANES)
        o_fox = _attention(_fox_attn_kernel, "fox_attn", fqa, fka, fv, g_fox, l, batch, seq,
                           FOX_HEADS, LANES)
        h = _ffn(o_mla, o_fox, o_lru, h, p_all, wo, rows3(g_ffn), wup, ffn_conv_w, rows3(ffn_conv_b), wd,
                 rows3(g_ple), wgate, wproj, g_final.reshape(1, -1), batch, seq, l, final=(l == depth - 1))
    return h.reshape(batch, seq, D_MODEL)
```

```python
import functools
import math

import jax
import jax.numpy as jnp
from jax import lax
from jax.experimental import pallas as pl
from jax.experimental.pallas import tpu as pltpu

F32 = jnp.float32
BF16 = jnp.bfloat16

D_MODEL = 1024
PLE_DIM = 256
MLA_HEADS = 4
MLA_NOPE = 64
MLA_ROPE = 32
MLA_HALF = MLA_ROPE // 2
MLA_V = 64
MLA_Q_RANK = 192
MLA_KV_RANK = 128
FOX_HEADS = 4
FOX_HEAD_DIM = 64
FOX_OUT = FOX_HEADS * FOX_HEAD_DIM
LRU_WIDTH = 512
LRU_BLOCKS = 8
LRU_BLOCK = LRU_WIDTH // LRU_BLOCKS
LRU_CONV = 4
LRU_C = 8.0
D_FF = 2816
FFN_CONV = 3
ROPE_THETA = 10000.0
EPS = 1e-6
LOG2E = math.log2(math.e)

LANES = 128
SUBLANES = 8
V7X_VMEM_BYTES = 64 << 20

ROW_TILE = 512
ATT_BLOCK = 512
FF_CHUNK = 256
N_FF_CHUNKS = D_FF // FF_CHUNK
FFN_ROW_BLOCK = 64

ZC_G0 = 0
ZC_KV = 256
ZC_FOX = 512
ZC_LRU = 1280
ZC_END = 2304
FLOGIT_LANE = MLA_Q_RANK - LANES

MLA_QSCALE = (MLA_NOPE + MLA_ROPE) ** -0.5 * LOG2E
FOX_QSCALE = FOX_HEAD_DIM ** -0.5 * LOG2E


def _vmem_limit(nbytes):
    return int(min(nbytes + (8 << 20), V7X_VMEM_BYTES - (4 << 20)))


def _const_spec(shape):
    nd = len(shape)
    return pl.BlockSpec(shape, lambda *_: (0,) * nd, pipeline_mode=pl.Buffered(1))


def _layer_spec(stacked, layer):
    nd = stacked.ndim - 1
    return pl.BlockSpec((None,) + stacked.shape[1:], lambda *_: (layer,) + (0,) * nd,
                        pipeline_mode=pl.Buffered(1))


def _rms(x, gain, n=None):
    n = x.shape[-1] if n is None else n
    return x * lax.rsqrt(jnp.sum(x * x, axis=-1, keepdims=True) * (1.0 / n) + EPS) * gain


def _dot(a, b):
    return jnp.dot(a, b, preferred_element_type=F32)


def _softplus(x):
    return jnp.maximum(x, 0.0) + jnp.log1p(jnp.exp(-jnp.abs(x)))


def _rope_table_kernel(pos_ref, freq_ref, sign_ref, cos_ref, sin_ref):
    ang = pos_ref[...].astype(F32) * freq_ref[...]
    cos_ref[...] = jnp.cos(ang)
    sin_ref[...] = jnp.sin(ang) * sign_ref[...]


def _rope_tables(positions):
    t = positions.size
    half = MLA_HALF
    freqs = ROPE_THETA ** (-jnp.arange(half, dtype=F32) / half)
    freq_row = jnp.tile(freqs, LANES // half)[None, :]
    sign_row = jnp.where(jnp.arange(LANES) < LANES // 2, -1.0, 1.0).astype(F32)[None, :]
    tm = ROW_TILE
    return pl.pallas_call(
        _rope_table_kernel,
        grid=(t // tm,),
        in_specs=[pl.BlockSpec((tm, 1), lambda i: (i, 0)),
                  _const_spec((1, LANES)), _const_spec((1, LANES))],
        out_specs=[pl.BlockSpec((tm, LANES), lambda i: (i, 0))] * 2,
        out_shape=[jax.ShapeDtypeStruct((t, LANES), F32)] * 2,
        name="rope_tables",
    )(positions.reshape(t, 1), freq_row, sign_row)


def _store_value_slots(ref, v_pairs):
    half = LANES // 2
    lane = lax.broadcasted_iota(jnp.int32, (v_pairs.shape[0], LANES), 1)
    for h in range(v_pairs.shape[1] // half):
        p, e = divmod(h, 2)
        own = (lane >= half * e) & (lane < half * (e + 1))
        ref[:, LANES * h:LANES * (h + 1)] = jnp.where(own, v_pairs[:, LANES * p:LANES * (p + 1)], 1.0).astype(BF16)


def _proj_kernel(h_ref, gmix_ref, win_ref, gqc_ref, wq_ref, gkvc_ref, wkv_ref, cos_ref, sin_ref,
                 bf_ref, cw_ref, cb_ref, wri_ref, br_ref, bi_ref, lam_ref, glru_ref,
                 qm_ref, km_ref, fqa_ref, fka_ref, vm_ref, fv_ref, o_ref, xbuf, hcar, ccar):
    @pl.when(pl.program_id(1) == 0)
    def _():
        xbuf[0:SUBLANES, :] = jnp.zeros((SUBLANES, LRU_WIDTH), F32)
        hcar[...] = jnp.zeros_like(hcar)
        ccar[...] = jnp.zeros_like(ccar)

    xn = _rms(h_ref[...], gmix_ref[...]).astype(BF16)
    cos = cos_ref[...]
    sin = sin_ref[...]

    def zcols(lo, hi):
        return _dot(xn, win_ref[:, lo:hi])

    def rotate(r):
        return r * cos + pltpu.roll(r, LANES // 2, axis=1) * sin

    lz = zcols(ZC_LRU, ZC_END)
    g0 = zcols(ZC_G0, ZC_KV)
    kvz = zcols(ZC_KV, ZC_FOX)
    f = zcols(ZC_FOX, ZC_LRU)
    fq = (f[:, :FOX_OUT] * FOX_QSCALE).astype(BF16)
    fk = f[:, FOX_OUT:2 * FOX_OUT].astype(BF16)
    _store_value_slots(fv_ref, f[:, 2 * FOX_OUT:])

    _scan_body(g0[:, LANES:], fq, fk, lz[:, :LRU_WIDTH], lz[:, LRU_WIDTH:],
               bf_ref, cw_ref, cb_ref, wri_ref, br_ref, bi_ref, lam_ref, glru_ref,
               fqa_ref, fka_ref, o_ref, xbuf, hcar, ccar)

    lane = lax.broadcasted_iota(jnp.int32, g0.shape, 1)
    qc = jnp.where(lane < MLA_Q_RANK, g0, 0.0)
    qcn = _rms(qc, gqc_ref[...], MLA_Q_RANK).astype(BF16)
    qq = _dot(qcn, wq_ref[...]) * MLA_QSCALE
    q_rot = rotate(qq[:, 2 * LANES:]).astype(BF16)
    qm_ref[:, 0:LANES] = qq[:, 0:LANES].astype(BF16)
    qm_ref[:, LANES:2 * LANES] = q_rot
    qm_ref[:, 2 * LANES:3 * LANES] = qq[:, LANES:2 * LANES].astype(BF16)
    qm_ref[:, 3 * LANES:] = q_rot

    kvn = _rms(kvz[:, :MLA_KV_RANK], gkvc_ref[...]).astype(BF16)
    kv = _dot(kvn, wkv_ref[...])
    k_rot = rotate(kvz[:, MLA_KV_RANK:]).astype(BF16)
    km_ref[:, 0:LANES] = kv[:, 0:LANES].astype(BF16)
    km_ref[:, LANES:2 * LANES] = k_rot
    km_ref[:, 2 * LANES:3 * LANES] = kv[:, LANES:2 * LANES].astype(BF16)
    km_ref[:, 3 * LANES:] = k_rot
    _store_value_slots(vm_ref, kv[:, 2 * LANES:])


def _proj_scan(h, cos, sin, proj_weights, scan_weights, batch, seq, layer):
    tm = ROW_TILE
    ns = seq // tm
    t = batch * seq
    rows = lambda c: pl.BlockSpec((tm, c), lambda b, s: (b * ns + s, 0))
    outs = [(4 * LANES, BF16), (4 * LANES, BF16), (FOX_HEADS * LANES, BF16), (FOX_HEADS * LANES, BF16),
            (MLA_HEADS * LANES, BF16), (FOX_HEADS * LANES, BF16), (LRU_WIDTH, BF16)]
    return pl.pallas_call(
        _proj_kernel,
        grid=(batch, ns),
        in_specs=([rows(D_MODEL)] + [_layer_spec(w, layer) for w in proj_weights] + [rows(LANES), rows(LANES)]
                  + [_layer_spec(w, layer) for w in scan_weights]),
        out_specs=[rows(c) for c, _ in outs],
        out_shape=[jax.ShapeDtypeStruct((t, c), d) for c, d in outs],
        scratch_shapes=[pltpu.VMEM((tm + SUBLANES, LRU_WIDTH), F32),
                        pltpu.VMEM((1, LRU_WIDTH), F32),
                        pltpu.VMEM((1, LANES), F32)],
        compiler_params=pltpu.CompilerParams(
            dimension_semantics=("parallel", "arbitrary"), vmem_limit_bytes=_vmem_limit(44 << 20)),
        name="proj_scan",
    )(h, *proj_weights, cos, sin, *scan_weights)


def _store_decay_slots(c2, fq, fk, fqa_ref, fka_ref):
    tm = c2.shape[0]
    half = LANES // 2
    lane = lax.broadcasted_iota(jnp.int32, (tm, LANES), 1)
    for h in range(FOX_HEADS):
        p, e = divmod(h, 2)
        ch = jnp.broadcast_to(c2[:, FLOGIT_LANE + h:FLOGIT_LANE + h + 1], (tm, LANES))
        hi = ch.astype(BF16).astype(F32)
        rest = ch - hi
        mid = rest.astype(BF16).astype(F32)
        lo = rest - mid
        parts = (hi, mid, lo)
        base = half * (1 - e)
        aug_q = jnp.where((lane >= base + 3) & (lane < base + 6), 1.0, 0.0)
        aug_k = jnp.where((lane >= base) & (lane < base + 3), 1.0, 0.0)
        for i, part in enumerate(parts):
            aug_q = jnp.where(lane == base + i, part, aug_q)
            aug_k = jnp.where(lane == base + 3 + i, -part, aug_k)
        own = (lane >= half * e) & (lane < half * (e + 1))
        cols = slice(LANES * p, LANES * (p + 1))
        slot = slice(LANES * h, LANES * (h + 1))
        fqa_ref[:, slot] = jnp.where(own, fq[:, cols], aug_q.astype(BF16))
        fka_ref[:, slot] = jnp.where(own, fk[:, cols], aug_k.astype(BF16))


def _scan_body(fl, fq, fk, lx, lg, bf_ref, cw_ref, cb_ref, wri_ref, br_ref, bi_ref, lam_ref, g_ref,
               fqa_ref, fka_ref, o_ref, xbuf, hcar, ccar):
    tm = lx.shape[0]
    halo = SUBLANES

    lane = lax.broadcasted_iota(jnp.int32, (tm, LANES), 1)
    row = lax.broadcasted_iota(jnp.int32, (tm, LANES), 0)
    valid = (lane >= FLOGIT_LANE) & (lane < FLOGIT_LANE + FOX_HEADS)
    c = jnp.where(valid, -_softplus(-(fl + bf_ref[...])), 0.0)
    shift = 1
    while shift < tm:
        c = c + jnp.where(row >= shift, pltpu.roll(c, shift, axis=0), 0.0)
        shift *= 2
    c = c + ccar[...]
    ccar[...] = c[tm - 1:tm, :]
    _store_decay_slots(c * LOG2E, fq, fk, fqa_ref, fka_ref)

    xbuf[halo:halo + tm, :] = lx
    xc = cb_ref[...] + cw_ref[LRU_CONV - 1:LRU_CONV, :] * lx
    for k in range(LRU_CONV - 1):
        back = LRU_CONV - 1 - k
        xc = xc + cw_ref[k:k + 1, :] * xbuf[halo - back:halo - back + tm, :]
    xbuf[0:halo, :] = xbuf[tm:tm + halo, :]

    xcb = xc.astype(BF16)
    half = LRU_WIDTH // 2
    gates = [_dot(xcb[:, j * half:(j + 1) * half], wri_ref[j]) for j in range(2)]
    r = jax.nn.sigmoid(jnp.concatenate([g[:, :half] for g in gates], axis=1) + br_ref[...])
    i = jax.nn.sigmoid(jnp.concatenate([g[:, half:] for g in gates], axis=1) + bi_ref[...])
    log_a = (-LRU_C) * r * _softplus(-lam_ref[...])
    a = jnp.exp(log_a)
    b = jnp.sqrt(-jnp.tanh(log_a) * (a * a + 1.0)) * (i * xc)

    rowl = lax.broadcasted_iota(jnp.int32, (tm, LRU_WIDTH), 0)
    shift = 1
    while shift < tm:
        keep = rowl >= shift
        a_sh = jnp.where(keep, pltpu.roll(a, shift, axis=0), 1.0)
        b_sh = jnp.where(keep, pltpu.roll(b, shift, axis=0), 0.0)
        b = a * b_sh + b
        a = a * a_sh
        shift *= 2
    h = b + a * hcar[...]
    hcar[...] = h[tm - 1:tm, :]

    o = h * jax.nn.gelu(lg, approximate=True)
    o_ref[...] = _rms(o, g_ref[...]).astype(BF16)


def _flash_heads(q_sc, load_k, load_v, qi, blk, s_sc, p_sc, m_sc, alpha_sc, acc_sc):
    nh = q_sc.shape[0]
    m_sc[...] = jnp.full(m_sc.shape, -jnp.inf, F32)
    acc_sc[...] = jnp.zeros(acc_sc.shape, F32)
    row = lax.broadcasted_iota(jnp.int32, (blk, blk), 0)
    col = lax.broadcasted_iota(jnp.int32, (blk, blk), 1)

    def scores(h, j, diagonal):
        s = lax.dot_general(q_sc[h], load_k(h, j), (((1,), (1,)), ((), ())), preferred_element_type=F32)
        if diagonal:
            s = jnp.where(col <= row, s, -jnp.inf)
        s_sc[h] = s

    def softmax(h):
        m_old = m_sc[h]
        m_blk = jnp.max(s_sc[h], axis=-1, keepdims=True)
        m_new = jnp.maximum(m_old, jnp.broadcast_to(m_blk, (blk, LANES)))
        m_sc[h] = m_new
        alpha_sc[h] = jnp.exp2(m_old - m_new)
        for c in range(0, blk, LANES):
            p_sc[h, :, c:c + LANES] = jnp.exp2(s_sc[h, :, c:c + LANES] - m_new).astype(BF16)

    def weighted_values(h, j):
        acc_sc[h] = alpha_sc[h] * acc_sc[h] + _dot(p_sc[h], load_v(h, j))

    def block(j, diagonal):
        for h in range(nh):
            scores(h, j, diagonal)
        for h in range(nh):
            softmax(h)
            weighted_values(h, j)

    def body(j, carry):
        block(j, False)
        return carry

    lax.fori_loop(0, qi, body, 0)
    block(qi, True)
    return [acc_sc[h] for h in range(nh)]


def _pair_norm_store(o_ref, accs, g_ref):
    half = LANES // 2
    lane = lax.broadcasted_iota(jnp.int32, accs[0].shape, 1)
    heads = [a / pltpu.roll(a, half, axis=1) for a in accs]
    pairs = [jnp.where(lane < half, heads[i], heads[i + 1]) for i in range(0, len(heads), 2)]
    n = len(pairs) * LANES
    ssq = sum(jnp.sum(p * p, axis=-1, keepdims=True) for p in pairs)
    rs = lax.rsqrt(ssq * (1.0 / n) + EPS)
    for i, p in enumerate(pairs):
        o_ref[:, i * LANES:(i + 1) * LANES] = (p * rs * g_ref[:, i * LANES:(i + 1) * LANES]).astype(BF16)


def _kv_rows(j, blk):
    return pl.ds(pl.multiple_of(j * blk, blk), blk)


def _mla_attn_kernel(q_ref, k_ref, v_ref, g_ref, o_ref, q_sc, *scratch):
    blk = q_ref.shape[0]
    lane = lax.broadcasted_iota(jnp.int32, (blk, 2 * LANES), 1)
    for h in range(MLA_HEADS):
        p, e = divmod(h, 2)
        x1 = LANES + MLA_HALF * h
        x2 = x1 + LANES // 2
        mine = (((lane >= MLA_NOPE * e) & (lane < MLA_NOPE * (e + 1)))
                | ((lane >= x1) & (lane < x1 + MLA_HALF))
                | ((lane >= x2) & (lane < x2 + MLA_HALF)))
        q_pair = q_ref[:, 2 * LANES * p:2 * LANES * (p + 1)]
        q_sc[h] = jnp.where(mine, q_pair, jnp.zeros_like(q_pair))
    load_k = lambda h, j: k_ref[_kv_rows(j, blk), 2 * LANES * (h // 2):2 * LANES * (h // 2 + 1)]
    load_v = lambda h, j: v_ref[_kv_rows(j, blk), LANES * h:LANES * (h + 1)]
    accs = _flash_heads(q_sc, load_k, load_v, pl.program_id(1), blk, *scratch)
    _pair_norm_store(o_ref, accs, g_ref)


def _fox_attn_kernel(q_ref, k_ref, v_ref, g_ref, o_ref, q_sc, *scratch):
    blk = q_ref.shape[0]
    for h in range(FOX_HEADS):
        q_sc[h] = q_ref[:, LANES * h:LANES * (h + 1)]
    load_k = lambda h, j: k_ref[_kv_rows(j, blk), LANES * h:LANES * (h + 1)]
    load_v = lambda h, j: v_ref[_kv_rows(j, blk), LANES * h:LANES * (h + 1)]
    accs = _flash_heads(q_sc, load_k, load_v, pl.program_id(1), blk, *scratch)
    _pair_norm_store(o_ref, accs, g_ref)


def _attention(kernel, name, q, k, v, gain, layer, batch, seq, heads, dk):
    blk = ATT_BLOCK
    nq = seq // blk
    tile = lambda a: pl.BlockSpec((None, blk, a.shape[-1]), lambda b, i: (b, i, 0))
    full = lambda a: pl.BlockSpec((None,) + a.shape[1:], lambda b, i: (b,) + (0,) * (a.ndim - 1))
    q3, k3, v3 = (a.reshape(batch, seq, a.shape[-1]) for a in (q, k, v))
    d_out = v.shape[-1] // 2
    out = pl.pallas_call(
        kernel,
        grid=(batch, nq),
        in_specs=[tile(q3), full(k3), full(v3), _layer_spec(gain, layer)],
        out_specs=pl.BlockSpec((None, blk, d_out), lambda b, i: (b, i, 0)),
        out_shape=jax.ShapeDtypeStruct((batch, seq, d_out), BF16),
        scratch_shapes=[pltpu.VMEM((heads, blk, dk), BF16),
                        pltpu.VMEM((heads, blk, blk), F32),
                        pltpu.VMEM((heads, blk, blk), BF16),
                        pltpu.VMEM((heads, blk, LANES), F32),
                        pltpu.VMEM((heads, blk, LANES), F32),
                        pltpu.VMEM((heads, blk, LANES), F32)],
        compiler_params=pltpu.CompilerParams(
            dimension_semantics=("parallel", "arbitrary"), vmem_limit_bytes=_vmem_limit(40 << 20)),
        name=name,
    )(q3, k3, v3, gain)
    return out.reshape(batch * seq, d_out)


def _ffn_kernel(om_ref, of_ref, ol_ref, h_ref, p_ref, wo_ref, gffn_ref, wup_ref, cw_ref, cb_ref, wd_ref,
                gple_ref, wgate_ref, wproj_ref, gfin_ref, out_ref,
                xn_sc, ubuf_g, ubuf_v, tail_g, tail_v, acc_ref, act_buf, *, final):
    tm = h_ref.shape[0]
    halo = SUBLANES

    @pl.when(pl.program_id(1) == 0)
    def _():
        tail_g[...] = jnp.zeros_like(tail_g)
        tail_v[...] = jnp.zeros_like(tail_v)

    a, b = om_ref.shape[1], om_ref.shape[1] + of_ref.shape[1]
    h1 = h_ref[...] + (_dot(om_ref[...], wo_ref[0:a, :]) + _dot(of_ref[...], wo_ref[a:b, :])
                       + _dot(ol_ref[...], wo_ref[b:, :]))
    acc_ref[...] = h1
    xn_sc[...] = _rms(h1, gffn_ref[...]).astype(BF16)

    def cols(c, part):
        lo = part * D_FF + c * FF_CHUNK
        return slice(lo, lo + FF_CHUNK)

    def up(c):
        xn = xn_sc[...]
        ubuf_g[c % 2, halo:halo + tm, :] = _dot(xn, wup_ref[:, cols(c, 0)])
        ubuf_v[c % 2, halo:halo + tm, :] = _dot(xn, wup_ref[:, cols(c, 1)])

    def conv(buf, r0, cw, cb):
        taps = [cw[k:k + 1, :] * buf[halo + r0 - (FFN_CONV - 1 - k):halo + r0 - (FFN_CONV - 1 - k) + FFN_ROW_BLOCK, :]
                for k in range(FFN_CONV)]
        return cb + taps[0] + taps[1] + taps[2]

    def gate(c):
        bg, bv = ubuf_g.at[c % 2], ubuf_v.at[c % 2]
        bg[0:halo, :] = tail_g[c]
        bv[0:halo, :] = tail_v[c]
        cwg, cwv = cw_ref[:, cols(c, 0)], cw_ref[:, cols(c, 1)]
        cbg, cbv = cb_ref[:, cols(c, 0)], cb_ref[:, cols(c, 1)]
        for r0 in range(0, tm, FFN_ROW_BLOCK):
            g = conv(bg, r0, cwg, cbg)
            v = conv(bv, r0, cwv, cbv)
            act_buf[r0:r0 + FFN_ROW_BLOCK, c * FF_CHUNK:(c + 1) * FF_CHUNK] = (g * jax.nn.sigmoid(g) * v).astype(BF16)
        tail_g[c] = bg[tm:tm + halo, :]
        tail_v[c] = bv[tm:tm + halo, :]

    up(0)
    for c in range(N_FF_CHUNKS):
        if c + 1 < N_FF_CHUNKS:
            up(c + 1)
        gate(c)

    h2 = acc_ref[...] + _dot(act_buf[...], wd_ref[...])
    ple_gate = jax.nn.sigmoid(_dot(_rms(h2, gple_ref[...]).astype(BF16), wgate_ref[...]))
    h3 = h2 + ple_gate * _dot(p_ref[...].astype(BF16), wproj_ref[...])
    out_ref[...] = _rms(h3, gfin_ref[...]) if final else h3


def _ffn(om, of, ol, h, p_all, wo, gffn, wup, cw, cb, wd, gple, wgate, wproj, gfin, batch, seq, layer, final):
    tm = ROW_TILE
    ns = seq // tm
    t = batch * seq
    rows = lambda a: pl.BlockSpec((tm, a.shape[1]), lambda b, s: (b * ns + s, 0))
    p_rows = pl.BlockSpec((tm, p_all.shape[1]), lambda b, s: (layer * (t // tm) + b * ns + s, 0))
    tiles = (om, of, ol, h)
    weights = (wo, gffn, wup, cw, cb, wd, gple, wgate, wproj)
    return pl.pallas_call(
        functools.partial(_ffn_kernel, final=final),
        grid=(batch, ns),
        in_specs=([rows(a) for a in tiles] + [p_rows] + [_layer_spec(w, layer) for w in weights]
                  + [_const_spec(gfin.shape)]),
        out_specs=rows(h),
        out_shape=jax.ShapeDtypeStruct((t, D_MODEL), F32),
        scratch_shapes=[pltpu.VMEM((tm, D_MODEL), BF16),
                        pltpu.VMEM((2, tm + SUBLANES, FF_CHUNK), F32),
                        pltpu.VMEM((2, tm + SUBLANES, FF_CHUNK), F32),
                        pltpu.VMEM((N_FF_CHUNKS, SUBLANES, FF_CHUNK), F32),
                        pltpu.VMEM((N_FF_CHUNKS, SUBLANES, FF_CHUNK), F32),
                        pltpu.VMEM((tm, D_MODEL), F32),
                        pltpu.VMEM((tm, D_FF), BF16)],
        compiler_params=pltpu.CompilerParams(
            dimension_semantics=("parallel", "arbitrary"), vmem_limit_bytes=_vmem_limit(48 << 20)),
        name="ffn_ple",
    )(*tiles, p_all, *weights, gfin)


def _pack_w_in(w):
    o_kvc = MLA_Q_RANK
    o_kr = o_kvc + MLA_KV_RANK
    o_fox = o_kr + MLA_ROPE
    o_fl = o_fox + 3 * FOX_OUT
    o_lru = o_fl + FOX_HEADS
    pad = jnp.zeros((w.shape[0], ZC_KV - MLA_Q_RANK - FOX_HEADS), w.dtype)
    kr1 = jnp.tile(w[:, o_kr:o_kr + MLA_HALF], (1, MLA_HEADS))
    kr2 = jnp.tile(w[:, o_kr + MLA_HALF:o_fox], (1, MLA_HEADS))
    return jnp.concatenate(
        [w[:, :MLA_Q_RANK], w[:, o_fl:o_lru], pad, w[:, o_kvc:o_kr], kr1, kr2, w[:, o_fox:o_fl], w[:, o_lru:]],
        axis=1).astype(BF16)


def _pack_w_uq(w):
    w3 = w.reshape(MLA_Q_RANK, MLA_HEADS, MLA_NOPE + MLA_ROPE)
    nope = w3[:, :, :MLA_NOPE].reshape(MLA_Q_RANK, -1)
    x1 = w3[:, :, MLA_NOPE:MLA_NOPE + MLA_HALF].reshape(MLA_Q_RANK, -1)
    x2 = w3[:, :, MLA_NOPE + MLA_HALF:].reshape(MLA_Q_RANK, -1)
    packed = jnp.concatenate([nope, x1, x2], axis=1)
    return jnp.pad(packed, ((0, 2 * LANES - MLA_Q_RANK), (0, 0))).astype(BF16)


def _pack_w_ukv(w):
    w3 = w.reshape(MLA_KV_RANK, MLA_HEADS, MLA_NOPE + MLA_V)
    return jnp.concatenate([w3[:, :, :MLA_NOPE].reshape(MLA_KV_RANK, -1),
                            w3[:, :, MLA_NOPE:].reshape(MLA_KV_RANK, -1)], axis=1).astype(BF16)


def _pack_gates(w_r, w_i):
    def block_diag(w):
        n = w.shape[0]
        eye = jnp.eye(n, dtype=w.dtype)
        return jnp.einsum("ncd,nm->ncmd", w, eye).reshape(n * LRU_BLOCK, n * LRU_BLOCK)
    hb = LRU_BLOCKS // 2
    return jnp.stack([jnp.concatenate([block_diag(w_r[j * hb:(j + 1) * hb]),
                                       block_diag(w_i[j * hb:(j + 1) * hb])], axis=1)
                      for j in range(2)]).astype(BF16)


def kernel(x, p, positions, g_mix, w_in, g_qc, w_uq, g_kvc, w_ukv, b_f, lru_conv_w, lru_conv_b, w_r, b_r, w_i, b_i, lru_lambda, g_out, w_o, g_ffn, w_up, ffn_conv_w, ffn_conv_b, w_down, g_ple, w_ple_gate, w_ple_proj, g_final):
    batch, seq, _ = x.shape
    depth = w_in.shape[0]
    t = batch * seq
    rows3 = lambda v: v.reshape(depth, 1, -1)
    bf16 = lambda w: w.astype(BF16)
    win, wq, wkv = jax.vmap(_pack_w_in)(w_in), jax.vmap(_pack_w_uq)(w_uq), jax.vmap(_pack_w_ukv)(w_ukv)
    gqc = rows3(jnp.pad(g_qc, ((0, 0), (0, 2 * LANES - MLA_Q_RANK))))
    bf_row = rows3(jnp.pad(b_f, ((0, 0), (FLOGIT_LANE, LANES - FLOGIT_LANE - FOX_HEADS))))
    wri = jax.vmap(_pack_gates)(w_r, w_i)
    g_mla, g_fox, g_lru = (rows3(g_out[:, :2 * LANES]), rows3(g_out[:, 2 * LANES:4 * LANES]),
                           rows3(g_out[:, 4 * LANES:]))
    wo, wup, wd, wgate, wproj = bf16(w_o), bf16(w_up), bf16(w_down), bf16(w_ple_gate), bf16(w_ple_proj)
    p_all = p.reshape(depth * t, PLE_DIM)

    cos, sin = _rope_tables(positions)
    h = x.reshape(t, D_MODEL)
    for l in range(depth):
        qm, km, fqa, fka, vm, fv, o_lru = _proj_scan(
            h, cos, sin, (rows3(g_mix), win, gqc, wq, rows3(g_kvc), wkv),
            (bf_row, lru_conv_w, rows3(lru_conv_b), wri, rows3(b_r), rows3(b_i), rows3(lru_lambda), g_lru),
            batch, seq, l)
        o_mla = _attention(_mla_attn_kernel, "mla_attn", qm, km, vm, g_mla, l, batch, seq,
                           MLA_HEADS, 2 * LANES)
        o_fox = _attention(_fox_attn_kernel, "fox_attn", fqa, fka, fv, g_fox, l, batch, seq,
                           FOX_HEADS, LANES)
        h = _ffn(o_mla, o_fox, o_lru, h, p_all, wo, rows3(g_ffn), wup, ffn_conv_w, rows3(ffn_conv_b), wd,
                 rows3(g_ple), wgate, wproj, g_final.reshape(1, -1), batch, seq, l, final=(l == depth - 1))
    return h.reshape(batch, seq, D_MODEL)
```

```python
import functools
import math

import jax
import jax.numpy as jnp
from jax import lax
from jax.experimental import pallas as pl
from jax.experimental.pallas import tpu as pltpu

F32 = jnp.float32
BF16 = jnp.bfloat16

D_MODEL = 1024
PLE_DIM = 256
MLA_HEADS = 4
MLA_NOPE = 64
MLA_ROPE = 32
MLA_HALF = MLA_ROPE // 2
MLA_V = 64
MLA_Q_RANK = 192
MLA_KV_RANK = 128
FOX_HEADS = 4
FOX_HEAD_DIM = 64
FOX_OUT = FOX_HEADS * FOX_HEAD_DIM
LRU_WIDTH = 512
LRU_BLOCKS = 8
LRU_BLOCK = LRU_WIDTH // LRU_BLOCKS
LRU_CONV = 4
LRU_C = 8.0
D_FF = 2816
FFN_CONV = 3
ROPE_THETA = 10000.0
EPS = 1e-6
LOG2E = math.log2(math.e)

LANES = 128
SUBLANES = 8
V7X_VMEM_BYTES = 64 << 20

ROW_TILE = 512
ATT_BLOCK = 512
FF_CHUNK = 256
N_FF_CHUNKS = D_FF // FF_CHUNK
FFN_ROW_BLOCK = 512

ZC_G0 = 0
ZC_KV = 256
ZC_FOX = 512
ZC_LRU = 1280
ZC_END = 2304
FLOGIT_LANE = MLA_Q_RANK - LANES

MLA_QSCALE = (MLA_NOPE + MLA_ROPE) ** -0.5 * LOG2E
FOX_QSCALE = FOX_HEAD_DIM ** -0.5 * LOG2E


def _vmem_limit(nbytes):
    return int(min(nbytes + (8 << 20), V7X_VMEM_BYTES - (4 << 20)))


def _const_spec(shape):
    nd = len(shape)
    return pl.BlockSpec(shape, lambda *_: (0,) * nd, pipeline_mode=pl.Buffered(1))


def _layer_spec(stacked, layer):
    nd = stacked.ndim - 1
    return pl.BlockSpec((None,) + stacked.shape[1:], lambda *_: (layer,) + (0,) * nd,
                        pipeline_mode=pl.Buffered(1))


def _rms(x, gain, n=None):
    n = x.shape[-1] if n is None else n
    return x * lax.rsqrt(jnp.sum(x * x, axis=-1, keepdims=True) * (1.0 / n) + EPS) * gain


def _dot(a, b):
    return jnp.dot(a, b, preferred_element_type=F32)


def _softplus(x):
    return jnp.maximum(x, 0.0) + jnp.log1p(jnp.exp(-jnp.abs(x)))


def _rope_table_kernel(pos_ref, freq_ref, sign_ref, cos_ref, sin_ref):
    ang = pos_ref[...].astype(F32) * freq_ref[...]
    cos_ref[...] = jnp.cos(ang)
    sin_ref[...] = jnp.sin(ang) * sign_ref[...]


def _rope_tables(positions):
    t = positions.size
    half = MLA_HALF
    freqs = ROPE_THETA ** (-jnp.arange(half, dtype=F32) / half)
    freq_row = jnp.tile(freqs, LANES // half)[None, :]
    sign_row = jnp.where(jnp.arange(LANES) < LANES // 2, -1.0, 1.0).astype(F32)[None, :]
    tm = ROW_TILE
    return pl.pallas_call(
        _rope_table_kernel,
        grid=(t // tm,),
        in_specs=[pl.BlockSpec((tm, 1), lambda i: (i, 0)),
                  _const_spec((1, LANES)), _const_spec((1, LANES))],
        out_specs=[pl.BlockSpec((tm, LANES), lambda i: (i, 0))] * 2,
        out_shape=[jax.ShapeDtypeStruct((t, LANES), F32)] * 2,
        name="rope_tables",
    )(positions.reshape(t, 1), freq_row, sign_row)


def _store_value_slots(ref, v_pairs):
    half = LANES // 2
    lane = lax.broadcasted_iota(jnp.int32, (v_pairs.shape[0], LANES), 1)
    for h in range(v_pairs.shape[1] // half):
        p, e = divmod(h, 2)
        own = (lane >= half * e) & (lane < half * (e + 1))
        ref[:, LANES * h:LANES * (h + 1)] = jnp.where(own, v_pairs[:, LANES * p:LANES * (p + 1)], 1.0).astype(BF16)


def _proj_kernel(h_ref, gmix_ref, win_ref, gqc_ref, wq_ref, gkvc_ref, wkv_ref, cos_ref, sin_ref,
                 bf_ref, cw_ref, cb_ref, wri_ref, br_ref, bi_ref, lam_ref, glru_ref,
                 qm_ref, km_ref, fqa_ref, fka_ref, vm_ref, fv_ref, o_ref, xbuf, hcar, ccar):
    @pl.when(pl.program_id(1) == 0)
    def _():
        xbuf[0:SUBLANES, :] = jnp.zeros((SUBLANES, LRU_WIDTH), F32)
        hcar[...] = jnp.zeros_like(hcar)
        ccar[...] = jnp.zeros_like(ccar)

    xn = _rms(h_ref[...], gmix_ref[...]).astype(BF16)
    cos = cos_ref[...]
    sin = sin_ref[...]

    def zcols(lo, hi):
        return _dot(xn, win_ref[:, lo:hi])

    def rotate(r):
        return r * cos + pltpu.roll(r, LANES // 2, axis=1) * sin

    lz = zcols(ZC_LRU, ZC_END)
    g0 = zcols(ZC_G0, ZC_KV)
    kvz = zcols(ZC_KV, ZC_FOX)
    f = zcols(ZC_FOX, ZC_LRU)
    fq = (f[:, :FOX_OUT] * FOX_QSCALE).astype(BF16)
    fk = f[:, FOX_OUT:2 * FOX_OUT].astype(BF16)
    _store_value_slots(fv_ref, f[:, 2 * FOX_OUT:])

    _scan_body(g0[:, LANES:], fq, fk, lz[:, :LRU_WIDTH], lz[:, LRU_WIDTH:],
               bf_ref, cw_ref, cb_ref, wri_ref, br_ref, bi_ref, lam_ref, glru_ref,
               fqa_ref, fka_ref, o_ref, xbuf, hcar, ccar)

    lane = lax.broadcasted_iota(jnp.int32, g0.shape, 1)
    qc = jnp.where(lane < MLA_Q_RANK, g0, 0.0)
    qcn = _rms(qc, gqc_ref[...], MLA_Q_RANK).astype(BF16)
    qq = _dot(qcn, wq_ref[...]) * MLA_QSCALE
    q_rot = rotate(qq[:, 2 * LANES:]).astype(BF16)
    qm_ref[:, 0:LANES] = qq[:, 0:LANES].astype(BF16)
    qm_ref[:, LANES:2 * LANES] = q_rot
    qm_ref[:, 2 * LANES:3 * LANES] = qq[:, LANES:2 * LANES].astype(BF16)
    qm_ref[:, 3 * LANES:] = q_rot

    kvn = _rms(kvz[:, :MLA_KV_RANK], gkvc_ref[...]).astype(BF16)
    kv = _dot(kvn, wkv_ref[...])
    k_rot = rotate(kvz[:, MLA_KV_RANK:]).astype(BF16)
    km_ref[:, 0:LANES] = kv[:, 0:LANES].astype(BF16)
    km_ref[:, LANES:2 * LANES] = k_rot
    km_ref[:, 2 * LANES:3 * LANES] = kv[:, LANES:2 * LANES].astype(BF16)
    km_ref[:, 3 * LANES:] = k_rot
    _store_value_slots(vm_ref, kv[:, 2 * LANES:])


def _proj_scan(h, cos, sin, proj_weights, scan_weights, batch, seq, layer):
    tm = ROW_TILE
    ns = seq // tm
    t = batch * seq
    rows = lambda c: pl.BlockSpec((tm, c), lambda b, s: (b * ns + s, 0))
    outs = [(4 * LANES, BF16), (4 * LANES, BF16), (FOX_HEADS * LANES, BF16), (FOX_HEADS * LANES, BF16),
            (MLA_HEADS * LANES, BF16), (FOX_HEADS * LANES, BF16), (LRU_WIDTH, BF16)]
    return pl.pallas_call(
        _proj_kernel,
        grid=(batch, ns),
        in_specs=([rows(D_MODEL)] + [_layer_spec(w, layer) for w in proj_weights] + [rows(LANES), rows(LANES)]
                  + [_layer_spec(w, layer) for w in scan_weights]),
        out_specs=[rows(c) for c, _ in outs],
        out_shape=[jax.ShapeDtypeStruct((t, c), d) for c, d in outs],
        scratch_shapes=[pltpu.VMEM((tm + SUBLANES, LRU_WIDTH), F32),
                        pltpu.VMEM((1, LRU_WIDTH), F32),
                        pltpu.VMEM((1, LANES), F32)],
        compiler_params=pltpu.CompilerParams(
            dimension_semantics=("parallel", "arbitrary"), vmem_limit_bytes=_vmem_limit(44 << 20)),
        name="proj_scan",
    )(h, *proj_weights, cos, sin, *scan_weights)


def _store_decay_slots(c2, fq, fk, fqa_ref, fka_ref):
    tm = c2.shape[0]
    half = LANES // 2
    lane = lax.broadcasted_iota(jnp.int32, (tm, LANES), 1)
    for h in range(FOX_HEADS):
        p, e = divmod(h, 2)
        ch = jnp.broadcast_to(c2[:, FLOGIT_LANE + h:FLOGIT_LANE + h + 1], (tm, LANES))
        hi = ch.astype(BF16).astype(F32)
        rest = ch - hi
        mid = rest.astype(BF16).astype(F32)
        lo = rest - mid
        parts = (hi, mid, lo)
        base = half * (1 - e)
        aug_q = jnp.where((lane >= base + 3) & (lane < base + 6), 1.0, 0.0)
        aug_k = jnp.where((lane >= base) & (lane < base + 3), 1.0, 0.0)
        for i, part in enumerate(parts):
            aug_q = jnp.where(lane == base + i, part, aug_q)
            aug_k = jnp.where(lane == base + 3 + i, -part, aug_k)
        own = (lane >= half * e) & (lane < half * (e + 1))
        cols = slice(LANES * p, LANES * (p + 1))
        slot = slice(LANES * h, LANES * (h + 1))
        fqa_ref[:, slot] = jnp.where(own, fq[:, cols], aug_q.astype(BF16))
        fka_ref[:, slot] = jnp.where(own, fk[:, cols], aug_k.astype(BF16))


def _scan_body(fl, fq, fk, lx, lg, bf_ref, cw_ref, cb_ref, wri_ref, br_ref, bi_ref, lam_ref, g_ref,
               fqa_ref, fka_ref, o_ref, xbuf, hcar, ccar):
    tm = lx.shape[0]
    halo = SUBLANES

    lane = lax.broadcasted_iota(jnp.int32, (tm, LANES), 1)
    row = lax.broadcasted_iota(jnp.int32, (tm, LANES), 0)
    valid = (lane >= FLOGIT_LANE) & (lane < FLOGIT_LANE + FOX_HEADS)
    c = jnp.where(valid, -_softplus(-(fl + bf_ref[...])), 0.0)
    shift = 1
    while shift < tm:
        c = c + jnp.where(row >= shift, pltpu.roll(c, shift, axis=0), 0.0)
        shift *= 2
    c = c + ccar[...]
    ccar[...] = c[tm - 1:tm, :]
    _store_decay_slots(c * LOG2E, fq, fk, fqa_ref, fka_ref)

    xbuf[halo:halo + tm, :] = lx
    xc = cb_ref[...] + cw_ref[LRU_CONV - 1:LRU_CONV, :] * lx
    for k in range(LRU_CONV - 1):
        back = LRU_CONV - 1 - k
        xc = xc + cw_ref[k:k + 1, :] * xbuf[halo - back:halo - back + tm, :]
    xbuf[0:halo, :] = xbuf[tm:tm + halo, :]

    xcb = xc.astype(BF16)
    half = LRU_WIDTH // 2
    gates = [_dot(xcb[:, j * half:(j + 1) * half], wri_ref[j]) for j in range(2)]
    r = jax.nn.sigmoid(jnp.concatenate([g[:, :half] for g in gates], axis=1) + br_ref[...])
    i = jax.nn.sigmoid(jnp.concatenate([g[:, half:] for g in gates], axis=1) + bi_ref[...])
    log_a = (-LRU_C) * r * _softplus(-lam_ref[...])
    a = jnp.exp(log_a)
    b = jnp.sqrt(-jnp.tanh(log_a) * (a * a + 1.0)) * (i * xc)

    rowl = lax.broadcasted_iota(jnp.int32, (tm, LRU_WIDTH), 0)
    shift = 1
    while shift < tm:
        keep = rowl >= shift
        a_sh = jnp.where(keep, pltpu.roll(a, shift, axis=0), 1.0)
        b_sh = jnp.where(keep, pltpu.roll(b, shift, axis=0), 0.0)
        b = a * b_sh + b
        a = a * a_sh
        shift *= 2
    h = b + a * hcar[...]
    hcar[...] = h[tm - 1:tm, :]

    o = h * jax.nn.gelu(lg, approximate=True)
    o_ref[...] = _rms(o, g_ref[...]).astype(BF16)


def _flash_heads(q_sc, load_k, load_v, qi, blk, s_sc, p_sc, m_sc, alpha_sc, acc_sc):
    nh = q_sc.shape[0]
    m_sc[...] = jnp.full(m_sc.shape, -jnp.inf, F32)
    acc_sc[...] = jnp.zeros(acc_sc.shape, F32)
    row = lax.broadcasted_iota(jnp.int32, (blk, blk), 0)
    col = lax.broadcasted_iota(jnp.int32, (blk, blk), 1)

    def scores(h, j, diagonal):
        s = lax.dot_general(q_sc[h], load_k(h, j), (((1,), (1,)), ((), ())), preferred_element_type=F32)
        if diagonal:
            s = jnp.where(col <= row, s, -jnp.inf)
        s_sc[h] = s

    def softmax(h):
        m_old = m_sc[h]
        m_blk = jnp.max(s_sc[h], axis=-1, keepdims=True)
        m_new = jnp.maximum(m_old, jnp.broadcast_to(m_blk, (blk, LANES)))
        m_sc[h] = m_new
        alpha_sc[h] = jnp.exp2(m_old - m_new)
        for c in range(0, blk, LANES):
            p_sc[h, :, c:c + LANES] = jnp.exp2(s_sc[h, :, c:c + LANES] - m_new).astype(BF16)

    def weighted_values(h, j):
        acc_sc[h] = alpha_sc[h] * acc_sc[h] + _dot(p_sc[h], load_v(h, j))

    def block(j, diagonal):
        for h in range(nh):
            scores(h, j, diagonal)
        for h in range(nh):
            softmax(h)
            weighted_values(h, j)

    def body(j, carry):
        block(j, False)
        return carry

    lax.fori_loop(0, qi, body, 0)
    block(qi, True)
    return [acc_sc[h] for h in range(nh)]


def _pair_norm_store(o_ref, accs, g_ref):
    half = LANES // 2
    lane = lax.broadcasted_iota(jnp.int32, accs[0].shape, 1)
    heads = [a / pltpu.roll(a, half, axis=1) for a in accs]
    pairs = [jnp.where(lane < half, heads[i], heads[i + 1]) for i in range(0, len(heads), 2)]
    n = len(pairs) * LANES
    ssq = sum(jnp.sum(p * p, axis=-1, keepdims=True) for p in pairs)
    rs = lax.rsqrt(ssq * (1.0 / n) + EPS)
    for i, p in enumerate(pairs):
        o_ref[:, i * LANES:(i + 1) * LANES] = (p * rs * g_ref[:, i * LANES:(i + 1) * LANES]).astype(BF16)


def _kv_rows(j, blk):
    return pl.ds(pl.multiple_of(j * blk, blk), blk)


def _mla_attn_kernel(q_ref, k_ref, v_ref, g_ref, o_ref, q_sc, *scratch):
    blk = q_ref.shape[0]
    lane = lax.broadcasted_iota(jnp.int32, (blk, 2 * LANES), 1)
    for h in range(MLA_HEADS):
        p, e = divmod(h, 2)
        x1 = LANES + MLA_HALF * h
        x2 = x1 + LANES // 2
        mine = (((lane >= MLA_NOPE * e) & (lane < MLA_NOPE * (e + 1)))
                | ((lane >= x1) & (lane < x1 + MLA_HALF))
                | ((lane >= x2) & (lane < x2 + MLA_HALF)))
        q_pair = q_ref[:, 2 * LANES * p:2 * LANES * (p + 1)]
        q_sc[h] = jnp.where(mine, q_pair, jnp.zeros_like(q_pair))
    load_k = lambda h, j: k_ref[_kv_rows(j, blk), 2 * LANES * (h // 2):2 * LANES * (h // 2 + 1)]
    load_v = lambda h, j: v_ref[_kv_rows(j, blk), LANES * h:LANES * (h + 1)]
    accs = _flash_heads(q_sc, load_k, load_v, pl.program_id(1), blk, *scratch)
    _pair_norm_store(o_ref, accs, g_ref)


def _fox_attn_kernel(q_ref, k_ref, v_ref, g_ref, o_ref, q_sc, *scratch):
    blk = q_ref.shape[0]
    for h in range(FOX_HEADS):
        q_sc[h] = q_ref[:, LANES * h:LANES * (h + 1)]
    load_k = lambda h, j: k_ref[_kv_rows(j, blk), LANES * h:LANES * (h + 1)]
    load_v = lambda h, j: v_ref[_kv_rows(j, blk), LANES * h:LANES * (h + 1)]
    accs = _flash_heads(q_sc, load_k, load_v, pl.program_id(1), blk, *scratch)
    _pair_norm_store(o_ref, accs, g_ref)


def _attention(kernel, name, q, k, v, gain, layer, batch, seq, heads, dk):
    blk = ATT_BLOCK
    nq = seq // blk
    tile = lambda a: pl.BlockSpec((None, blk, a.shape[-1]), lambda b, i: (b, i, 0))
    full = lambda a: pl.BlockSpec((None,) + a.shape[1:], lambda b, i: (b,) + (0,) * (a.ndim - 1))
    q3, k3, v3 = (a.reshape(batch, seq, a.shape[-1]) for a in (q, k, v))
    d_out = v.shape[-1] // 2
    out = pl.pallas_call(
        kernel,
        grid=(batch, nq),
        in_specs=[tile(q3), full(k3), full(v3), _layer_spec(gain, layer)],
        out_specs=pl.BlockSpec((None, blk, d_out), lambda b, i: (b, i, 0)),
        out_shape=jax.ShapeDtypeStruct((batch, seq, d_out), BF16),
        scratch_shapes=[pltpu.VMEM((heads, blk, dk), BF16),
                        pltpu.VMEM((heads, blk, blk), F32),
                        pltpu.VMEM((heads, blk, blk), BF16),
                        pltpu.VMEM((heads, blk, LANES), F32),
                        pltpu.VMEM((heads, blk, LANES), F32),
                        pltpu.VMEM((heads, blk, LANES), F32)],
        compiler_params=pltpu.CompilerParams(
            dimension_semantics=("parallel", "arbitrary"), vmem_limit_bytes=_vmem_limit(40 << 20)),
        name=name,
    )(q3, k3, v3, gain)
    return out.reshape(batch * seq, d_out)


def _ffn_kernel(om_ref, of_ref, ol_ref, h_ref, p_ref, wo_ref, gffn_ref, wup_ref, cw_ref, cb_ref, wd_ref,
                gple_ref, wgate_ref, wproj_ref, gfin_ref, out_ref,
                xn_sc, ubuf_g, ubuf_v, tail_g, tail_v, acc_ref, act_buf, *, final):
    tm = h_ref.shape[0]
    halo = SUBLANES

    @pl.when(pl.program_id(1) == 0)
    def _():
        tail_g[...] = jnp.zeros_like(tail_g)
        tail_v[...] = jnp.zeros_like(tail_v)

    a, b = om_ref.shape[1], om_ref.shape[1] + of_ref.shape[1]
    h1 = h_ref[...] + (_dot(om_ref[...], wo_ref[0:a, :]) + _dot(of_ref[...], wo_ref[a:b, :])
                       + _dot(ol_ref[...], wo_ref[b:, :]))
    acc_ref[...] = h1
    xn_sc[...] = _rms(h1, gffn_ref[...]).astype(BF16)

    def cols(c, part):
        lo = part * D_FF + c * FF_CHUNK
        return slice(lo, lo + FF_CHUNK)

    def up(c):
        xn = xn_sc[...]
        ubuf_g[c % 2, halo:halo + tm, :] = _dot(xn, wup_ref[:, cols(c, 0)])
        ubuf_v[c % 2, halo:halo + tm, :] = _dot(xn, wup_ref[:, cols(c, 1)])

    def conv(buf, r0, cw, cb):
        taps = [cw[k:k + 1, :] * buf[halo + r0 - (FFN_CONV - 1 - k):halo + r0 - (FFN_CONV - 1 - k) + FFN_ROW_BLOCK, :]
                for k in range(FFN_CONV)]
        return cb + taps[0] + taps[1] + taps[2]

    def gate(c):
        bg, bv = ubuf_g.at[c % 2], ubuf_v.at[c % 2]
        bg[0:halo, :] = tail_g[c]
        bv[0:halo, :] = tail_v[c]
        cwg, cwv = cw_ref[:, cols(c, 0)], cw_ref[:, cols(c, 1)]
        cbg, cbv = cb_ref[:, cols(c, 0)], cb_ref[:, cols(c, 1)]
        for r0 in range(0, tm, FFN_ROW_BLOCK):
            g = conv(bg, r0, cwg, cbg)
            v = conv(bv, r0, cwv, cbv)
            act_buf[r0:r0 + FFN_ROW_BLOCK, c * FF_CHUNK:(c + 1) * FF_CHUNK] = (g * jax.nn.sigmoid(g) * v).astype(BF16)
        tail_g[c] = bg[tm:tm + halo, :]
        tail_v[c] = bv[tm:tm + halo, :]

    up(0)
    for c in range(N_FF_CHUNKS):
        if c + 1 < N_FF_CHUNKS:
            up(c + 1)
        gate(c)

    h2 = acc_ref[...] + _dot(act_buf[...], wd_ref[...])
    ple_gate = jax.nn.sigmoid(_dot(_rms(h2, gple_ref[...]).astype(BF16), wgate_ref[...]))
    h3 = h2 + ple_gate * _dot(p_ref[...].astype(BF16), wproj_ref[...])
    out_ref[...] = _rms(h3, gfin_ref[...]) if final else h3


def _ffn(om, of, ol, h, p_all, wo, gffn, wup, cw, cb, wd, gple, wgate, wproj, gfin, batch, seq, layer, final):
    tm = ROW_TILE
    ns = seq // tm
    t = batch * seq
    rows = lambda a: pl.BlockSpec((tm, a.shape[1]), lambda b, s: (b * ns + s, 0))
    p_rows = pl.BlockSpec((tm, p_all.shape[1]), lambda b, s: (layer * (t // tm) + b * ns + s, 0))
    tiles = (om, of, ol, h)
    weights = (wo, gffn, wup, cw, cb, wd, gple, wgate, wproj)
    return pl.pallas_call(
        functools.partial(_ffn_kernel, final=final),
        grid=(batch, ns),
        in_specs=([rows(a) for a in tiles] + [p_rows] + [_layer_spec(w, layer) for w in weights]
                  + [_const_spec(gfin.shape)]),
        out_specs=rows(h),
        out_shape=jax.ShapeDtypeStruct((t, D_MODEL), F32),
        scratch_shapes=[pltpu.VMEM((tm, D_MODEL), BF16),
                        pltpu.VMEM((2, tm + SUBLANES, FF_CHUNK), F32),
                        pltpu.VMEM((2, tm + SUBLANES, FF_CHUNK), F32),
                        pltpu.VMEM((N_FF_CHUNKS, SUBLANES, FF_CHUNK), F32),
                        pltpu.VMEM((N_FF_CHUNKS, SUBLANES, FF_CHUNK), F32),
                        pltpu.VMEM((tm, D_MODEL), F32),
                        pltpu.VMEM((tm, D_FF), BF16)],
        compiler_params=pltpu.CompilerParams(
            dimension_semantics=("parallel", "arbitrary"), vmem_limit_bytes=_vmem_limit(48 << 20)),
        name="ffn_ple",
    )(*tiles, p_all, *weights, gfin)


def _pack_w_in(w):
    o_kvc = MLA_Q_RANK
    o_kr = o_kvc + MLA_KV_RANK
    o_fox = o_kr + MLA_ROPE
    o_fl = o_fox + 3 * FOX_OUT
    o_lru = o_fl + FOX_HEADS
    pad = jnp.zeros((w.shape[0], ZC_KV - MLA_Q_RANK - FOX_HEADS), w.dtype)
    kr1 = jnp.tile(w[:, o_kr:o_kr + MLA_HALF], (1, MLA_HEADS))
    kr2 = jnp.tile(w[:, o_kr + MLA_HALF:o_fox], (1, MLA_HEADS))
    return jnp.concatenate(
        [w[:, :MLA_Q_RANK], w[:, o_fl:o_lru], pad, w[:, o_kvc:o_kr], kr1, kr2, w[:, o_fox:o_fl], w[:, o_lru:]],
        axis=1).astype(BF16)


def _pack_w_uq(w):
    w3 = w.reshape(MLA_Q_RANK, MLA_HEADS, MLA_NOPE + MLA_ROPE)
    nope = w3[:, :, :MLA_NOPE].reshape(MLA_Q_RANK, -1)
    x1 = w3[:, :, MLA_NOPE:MLA_NOPE + MLA_HALF].reshape(MLA_Q_RANK, -1)
    x2 = w3[:, :, MLA_NOPE + MLA_HALF:].reshape(MLA_Q_RANK, -1)
    packed = jnp.concatenate([nope, x1, x2], axis=1)
    return jnp.pad(packed, ((0, 2 * LANES - MLA_Q_RANK), (0, 0))).astype(BF16)


def _pack_w_ukv(w):
    w3 = w.reshape(MLA_KV_RANK, MLA_HEADS, MLA_NOPE + MLA_V)
    return jnp.concatenate([w3[:, :, :MLA_NOPE].reshape(MLA_KV_RANK, -1),
                            w3[:, :, MLA_NOPE:].reshape(MLA_KV_RANK, -1)], axis=1).astype(BF16)


def _pack_gates(w_r, w_i):
    def block_diag(w):
        n = w.shape[0]
        eye = jnp.eye(n, dtype=w.dtype)
        return jnp.einsum("ncd,nm->ncmd", w, eye).reshape(n * LRU_BLOCK, n * LRU_BLOCK)
    hb = LRU_BLOCKS // 2
    return jnp.stack([jnp.concatenate([block_diag(w_r[j * hb:(j + 1) * hb]),
                                       block_diag(w_i[j * hb:(j + 1) * hb])], axis=1)
                      for j in range(2)]).astype(BF16)


def kernel(x, p, positions, g_mix, w_in, g_qc, w_uq, g_kvc, w_ukv, b_f, lru_conv_w, lru_conv_b, w_r, b_r, w_i, b_i, lru_lambda, g_out, w_o, g_ffn, w_up, ffn_conv_w, ffn_conv_b, w_down, g_ple, w_ple_gate, w_ple_proj, g_final):
    batch, seq, _ = x.shape
    depth = w_in.shape[0]
    t = batch * seq
    rows3 = lambda v: v.reshape(depth, 1, -1)
    bf16 = lambda w: w.astype(BF16)
    win, wq, wkv = jax.vmap(_pack_w_in)(w_in), jax.vmap(_pack_w_uq)(w_uq), jax.vmap(_pack_w_ukv)(w_ukv)
    gqc = rows3(jnp.pad(g_qc, ((0, 0), (0, 2 * LANES - MLA_Q_RANK))))
    bf_row = rows3(jnp.pad(b_f, ((0, 0), (FLOGIT_LANE, LANES - FLOGIT_LANE - FOX_HEADS))))
    wri = jax.vmap(_pack_gates)(w_r, w_i)
    g_mla, g_fox, g_lru = (rows3(g_out[:, :2 * LANES]), rows3(g_out[:, 2 * LANES:4 * LANES]),
                           rows3(g_out[:, 4 * LANES:]))
    wo, wup, wd, wgate, wproj = bf16(w_o), bf16(w_up), bf16(w_down), bf16(w_ple_gate), bf16(w_ple_proj)
    p_all = p.reshape(depth * t, PLE_DIM)

    cos, sin = _rope_tables(positions)
    h = x.reshape(t, D_MODEL)
    for l in range(depth):
        qm, km, fqa, fka, vm, fv, o_lru = _proj_scan(
            h, cos, sin, (rows3(g_mix), win, gqc, wq, rows3(g_kvc), wkv),
            (bf_row, lru_conv_w, rows3(lru_conv_b), wri, rows3(b_r), rows3(b_i), rows3(lru_lambda), g_lru),
            batch, seq, l)
        o_mla = _attention(_mla_attn_kernel, "mla_attn", qm, km, vm, g_mla, l, batch, seq,
                           MLA_HEADS, 2 * LANES)
        o_fox = _attention(_fox_attn_kernel, "fox_attn", fqa, fka, fv, g_fox, l, batch, seq,
                           FOX_HEADS, LANES)
        h = _ffn(o_mla, o_fox, o_lru, h, p_all, wo, rows3(g_ffn), wup, ffn_conv_w, rows3(ffn_conv_b), wd,
                 rows3(g_ple), wgate, wproj, g_final.reshape(1, -1), batch, seq, l, final=(l == depth - 1))
    return h.reshape(batch, seq, D_MODEL)
```
